```python
import jax, jax.numpy as jnp
from jax import lax
import numpy as np

D_MODEL = 1024
BATCH = 8
SEQ = 2048
DEPTH = 2

CHUNK = 64
N_MIXERS = 2
N_A_LAYERS = (DEPTH + 1) // 2
N_B_LAYERS = DEPTH // 2
RMS_EPS = 1e-6

A_HEADS = 16
A_HEAD_DIM = D_MODEL // A_HEADS
A_KV_RANK = D_MODEL // 4
A_IDX_HEADS = 8
A_IDX_DIM = 64
A_TOPK_MAX = 256
A_QBLOCK = 128
A_SPLITS = [A_HEADS * A_HEAD_DIM,
            A_HEADS * A_HEAD_DIM + A_KV_RANK,
            A_HEADS * A_HEAD_DIM + A_KV_RANK + A_IDX_HEADS * A_IDX_DIM,
            A_HEADS * A_HEAD_DIM + A_KV_RANK + A_IDX_HEADS * A_IDX_DIM + A_IDX_DIM]
A_IN_COLS = A_SPLITS[-1] + A_IDX_HEADS

B_K_HEADS = 8
B_V_HEADS = 16
B_HEAD_DIM = 128
B_CONV = 4
B_KEY_WIDTH = B_K_HEADS * B_HEAD_DIM
B_VAL_WIDTH = B_V_HEADS * B_HEAD_DIM
B_CONV_WIDTH = 2 * B_KEY_WIDTH + B_VAL_WIDTH
B_IN_COLS = B_CONV_WIDTH + B_VAL_WIDTH + 2 * B_V_HEADS

P_HEADS = 8
P_N_KEYS = 128
P_N_EXPERTS = P_N_KEYS * P_N_KEYS
P_QUERY_DIM = 256
P_HALF = P_QUERY_DIM // 2
P_TOPK = 16
P_TOKEN_BLOCK = 128

kernel_name = "hybrid_dsa_gdn_peer_adaln"


def rms_norm(x, gain=None):
    xf = x.astype(jnp.float32)
    y = xf * lax.rsqrt(jnp.mean(xf * xf, axis=-1, keepdims=True) + RMS_EPS)
    if gain is not None:
        y = y * gain.astype(jnp.float32)
    return y.astype(x.dtype)


def l2_norm(x):
    xf = x.astype(jnp.float32)
    return xf * lax.rsqrt(jnp.sum(xf * xf, axis=-1, keepdims=True) + RMS_EPS)


def modulate(h, shift, scale):
    return h * (1.0 + scale[:, None, :]) + shift[:, None, :]


def dsa_mixer(h, w_in, kv_norm, w_uk, w_uv, w_out):
    b, t, _ = h.shape
    proj = h @ w_in
    q, c_kv, q_idx, k_idx, w_idx = jnp.split(proj, A_SPLITS, axis=-1)
    q = q.reshape(b, t, A_HEADS, A_HEAD_DIM)
    c_kv = rms_norm(c_kv, kv_norm)
    q_lat = jnp.einsum('bthd,rhd->bthr', q, w_uk) * (A_HEAD_DIM ** -0.5)
    q_idx = q_idx.reshape(b, t, A_IDX_HEADS, A_IDX_DIM)
    w_idx = w_idx * ((A_IDX_HEADS ** -0.5) * (A_IDX_DIM ** -0.5))
    n_sel = min(A_TOPK_MAX, t // 4)
    key_chunk = jnp.arange(t) // CHUNK

    def block(i):
        start = i * A_QBLOCK
        qi = lax.dynamic_slice_in_dim(q_idx, start, A_QBLOCK, axis=1)
        wi = lax.dynamic_slice_in_dim(w_idx, start, A_QBLOCK, axis=1)
        ql = lax.dynamic_slice_in_dim(q_lat, start, A_QBLOCK, axis=1)
        q_chunk = (start + jnp.arange(A_QBLOCK)) // CHUNK
        admissible = key_chunk[None, :] <= q_chunk[:, None]
        raw = jax.nn.relu(jnp.einsum('bqhd,bsd->bqhs', qi, k_idx))
        score = jnp.einsum('bqh,bqhs->bqs', wi, raw).astype(jnp.float32)
        score = jnp.where(admissible[None], score, -jnp.inf)
        top_val, top_idx = lax.top_k(score, n_sel)
        valid = jnp.isfinite(top_val)
        c_sel = jax.vmap(lambda cb, ib: cb[ib])(c_kv, top_idx)
        logits = jnp.einsum('bqhr,bqkr->bqhk', ql, c_sel).astype(jnp.float32)
        logits = jnp.where(valid[:, :, None, :], logits, -jnp.inf)
        p = jax.nn.softmax(logits, axis=-1).astype(h.dtype)
        o_lat = jnp.einsum('bqhk,bqkr->bqhr', p, c_sel)
        return jnp.einsum('bqhr,rhd->bqhd', o_lat, w_uv)

    out = lax.map(block, jnp.arange(t // A_QBLOCK))
    out = out.transpose(1, 0, 2, 3, 4).reshape(b, t, A_HEADS * A_HEAD_DIM)
    return out @ w_out


def chunk_gated_delta_rule(q, k, v, g, beta):
    b, t, nh, dk = q.shape
    dv = v.shape[-1]
    n = t // CHUNK
    f32 = jnp.float32

    def to_chunks(a):
        return a.astype(f32).reshape(b, n, CHUNK, nh, -1).transpose(0, 3, 1, 2, 4)

    def to_chunks_s(a):
        return a.astype(f32).reshape(b, n, CHUNK, nh).transpose(0, 3, 1, 2)

    q = to_chunks(q) * (dk ** -0.5)
    k = to_chunks(k)
    v = to_chunks(v)
    beta = to_chunks_s(beta)
    g = jnp.cumsum(to_chunks_s(g), axis=-1)
    tril = jnp.tril(jnp.ones((CHUNK, CHUNK), bool))
    strict = jnp.tril(jnp.ones((CHUNK, CHUNK), bool), -1)
    decay = jnp.exp(jnp.where(tril, g[..., :, None] - g[..., None, :], -jnp.inf))
    k_beta = k * beta[..., None]
    a_low = jnp.where(strict, jnp.einsum('bhncd,bhnsd->bhncs', k_beta, k) * decay, 0.0)
    lhs = a_low + jnp.eye(CHUNK, dtype=f32)
    rhs = jnp.concatenate([v * beta[..., None], k_beta * jnp.exp(g)[..., None]], axis=-1)
    sol = lax.linalg.triangular_solve(lhs, rhs, left_side=True, lower=True, unit_diagonal=True)
    u, w = sol[..., :dv], sol[..., dv:]
    attn = jnp.where(tril, jnp.einsum('bhncd,bhnsd->bhncs', q, k) * decay, 0.0)

    def step(state, inp):
        q_i, k_i, u_i, w_i, g_i, a_i = inp
        v_new = u_i - jnp.einsum('bhcd,bhde->bhce', w_i, state)
        o = jnp.einsum('bhcd,bhde->bhce', q_i * jnp.exp(g_i)[..., None], state) + \
            jnp.einsum('bhcs,bhse->bhce', a_i, v_new)
        g_last = g_i[..., -1]
        state = state * jnp.exp(g_last)[..., None, None] + jnp.einsum(
            'bhcd,bhce->bhde', k_i * jnp.exp(g_last[..., None] - g_i)[..., None], v_new)
        return state, o

    xs = tuple(jnp.moveaxis(a, 2, 0) for a in (q, k, u, w, g, attn))
    state0 = jnp.zeros((b, nh, dk, dv), f32)
    _, o = lax.scan(step, state0, xs)
    return o.transpose(1, 0, 3, 2, 4).reshape(b, t, nh, dv)


def gated_deltanet_mixer(h, w_in, conv_w, a_log, dt_bias, out_norm, w_out):
    b, t, _ = h.shape
    proj = h @ w_in
    qkv, z, beta_raw, a_raw = jnp.split(
        proj, [B_CONV_WIDTH, B_CONV_WIDTH + B_VAL_WIDTH, B_CONV_WIDTH + B_VAL_WIDTH + B_V_HEADS], axis=-1)
    qkv = lax.conv_general_dilated(qkv, conv_w.astype(qkv.dtype), window_strides=(1,),
                                   padding=[(B_CONV - 1, 0)],
                                   dimension_numbers=('NWC', 'WIO', 'NWC'),
                                   feature_group_count=B_CONV_WIDTH)
    qkv = jax.nn.silu(qkv)
    q, k, v = jnp.split(qkv, [B_KEY_WIDTH, 2 * B_KEY_WIDTH], axis=-1)
    rep = B_V_HEADS // B_K_HEADS
    q = jnp.repeat(l2_norm(q.reshape(b, t, B_K_HEADS, B_HEAD_DIM)), rep, axis=2)
    k = jnp.repeat(l2_norm(k.reshape(b, t, B_K_HEADS, B_HEAD_DIM)), rep, axis=2)
    v = v.reshape(b, t, B_V_HEADS, B_HEAD_DIM)
    beta = jax.nn.sigmoid(beta_raw.astype(jnp.float32))
    g = -jnp.exp(a_log.astype(jnp.float32)) * jax.nn.softplus(
        a_raw.astype(jnp.float32) + dt_bias.astype(jnp.float32))
    o = chunk_gated_delta_rule(q, k, v, g, beta)
    z = z.reshape(b, t, B_V_HEADS, B_HEAD_DIM).astype(jnp.float32)
    o = rms_norm(o, out_norm) * jax.nn.silu(z)
    return o.reshape(b, t, B_VAL_WIDTH).astype(h.dtype) @ w_out


def peer_ffn(h, w_q, sub_keys, u_tab, v_tab):
    b, t, d = h.shape
    n_tok = b * t
    x = h.reshape(n_tok, d)
    q = (x @ w_q).reshape(n_tok, P_HEADS, 2, P_HALF)
    s = jnp.einsum('nhpc,hpkc->nhpk', q, sub_keys).astype(jnp.float32)
    v1, i1 = lax.top_k(s[:, :, 0], P_TOPK)
    v2, i2 = lax.top_k(s[:, :, 1], P_TOPK)
    cand = (v1[..., :, None] + v2[..., None, :]).reshape(n_tok, P_HEADS, P_TOPK * P_TOPK)
    cand_idx = (i1[..., :, None] * P_N_KEYS + i2[..., None, :]).reshape(n_tok, P_HEADS, P_TOPK * P_TOPK)
    top_s, pos = lax.top_k(cand, P_TOPK)
    expert = jnp.take_along_axis(cand_idx, pos, axis=-1)
    gate = jax.nn.softmax(top_s, axis=-1).astype(h.dtype)
    n_blk = n_tok // P_TOKEN_BLOCK

    def blk(args):
        xb, eb, gb = args
        u = jnp.take(u_tab, eb, axis=0)
        act = jax.nn.gelu(jnp.einsum('td,thkd->thk', xb, u), approximate=False)
        vv = jnp.take(v_tab, eb, axis=0)
        return jnp.einsum('thk,thkd->td', gb * act, vv)

    y = lax.map(blk, (x.reshape(n_blk, P_TOKEN_BLOCK, d),
                      expert.reshape(n_blk, P_TOKEN_BLOCK, P_HEADS, P_TOPK),
                      gate.reshape(n_blk, P_TOKEN_BLOCK, P_HEADS, P_TOPK)))
    return y.reshape(b, t, d)


def setup_inputs(seed: int = 0) -> dict:
    key = jax.random.key(seed)
    ks = jax.random.split(key, 20)
    nrm = jax.random.normal
    D = D_MODEL
    dt = jnp.exp(jax.random.uniform(ks[10], (N_B_LAYERS, B_V_HEADS),
                                    minval=float(np.log(1e-3)), maxval=float(np.log(0.1))))
    return {
        "x": nrm(ks[0], (BATCH, SEQ, D), jnp.float32),
        "c": nrm(ks[1], (BATCH, D), jnp.float32),
        "a_w_in": nrm(ks[2], (N_A_LAYERS, D, A_IN_COLS)) * D ** -0.5,
        "a_kv_norm": 1.0 + 0.02 * nrm(ks[3], (N_A_LAYERS, A_KV_RANK)),
        "a_w_uk": nrm(ks[4], (N_A_LAYERS, A_KV_RANK, A_HEADS, A_HEAD_DIM)) * A_KV_RANK ** -0.5,
        "a_w_uv": nrm(ks[5], (N_A_LAYERS, A_KV_RANK, A_HEADS, A_HEAD_DIM)) * A_KV_RANK ** -0.5,
        "a_w_out": nrm(ks[6], (N_A_LAYERS, A_HEADS * A_HEAD_DIM, D)) * (A_HEADS * A_HEAD_DIM) ** -0.5,
        "b_w_in": nrm(ks[7], (N_B_LAYERS, D, B_IN_COLS)) * D ** -0.5,
        "b_conv": nrm(ks[8], (N_B_LAYERS, B_CONV, 1, B_CONV_WIDTH)) * B_CONV ** -0.5,
        "b_a_log": jnp.log(jax.random.uniform(ks[9], (N_B_LAYERS, B_V_HEADS), minval=1.0, maxval=16.0)),
        "b_dt_bias": dt + jnp.log(-jnp.expm1(-dt)),
        "b_out_norm": 1.0 + 0.02 * nrm(ks[11], (N_B_LAYERS, B_HEAD_DIM)),
        "b_w_out": nrm(ks[12], (N_B_LAYERS, B_VAL_WIDTH, D)) * B_VAL_WIDTH ** -0.5,
        "p_w_q": nrm(ks[13], (DEPTH, D, P_HEADS * P_QUERY_DIM)) * D ** -0.5,
        "p_sub_keys": nrm(ks[14], (DEPTH, P_HEADS, 2, P_N_KEYS, P_HALF)) * P_HALF ** -0.5,
        "p_u": nrm(ks[15], (DEPTH, P_N_EXPERTS, D)) * D ** -0.5,
        "p_v": nrm(ks[16], (DEPTH, P_N_EXPERTS, D)),
        "ada_w": nrm(ks[17], (DEPTH, D, 6 * D)) * (0.5 * D ** -0.5),
        "ada_b": 0.02 * nrm(ks[18], (DEPTH, 6 * D)),
        "final_norm": 1.0 + 0.02 * nrm(ks[19], (D,)),
    }


def reference(x, c, a_w_in, a_kv_norm, a_w_uk, a_w_uv, a_w_out, b_w_in, b_conv, b_a_log,
              b_dt_bias, b_out_norm, b_w_out, p_w_q, p_sub_keys, p_u, p_v, ada_w, ada_b,
              final_norm):
    cond = jax.nn.silu(c)
    for layer in range(DEPTH):
        mod = cond @ ada_w[layer] + ada_b[layer]
        sh_m, sc_m, g_m, sh_f, sc_f, g_f = jnp.split(mod, 6, axis=-1)
        h = modulate(rms_norm(x), sh_m, sc_m)
        j = layer // N_MIXERS
        if layer % N_MIXERS == 0:
            mix = dsa_mixer(h, a_w_in[j], a_kv_norm[j], a_w_uk[j], a_w_uv[j], a_w_out[j])
        else:
            mix = gated_deltanet_mixer(h, b_w_in[j], b_conv[j], b_a_log[j], b_dt_bias[j],
                                       b_out_norm[j], b_w_out[j])
        x = x + g_m[:, None, :] * mix
        h = modulate(rms_norm(x), sh_f, sc_f)
        x = x + g_f[:, None, :] * peer_ffn(h, p_w_q[layer], p_sub_keys[layer], p_u[layer], p_v[layer])
    return rms_norm(x, final_norm)
```

```python
import functools

import jax
import jax.numpy as jnp
from jax import lax
from jax.experimental import pallas as pl
from jax.experimental.pallas import tpu as pltpu

F32 = jnp.float32
BF16 = jnp.bfloat16
I32 = jnp.int32
HIGHEST = lax.Precision.HIGHEST

RMS_EPS = 1e-6
CHUNK = 64
LANES = 128
NEG_BIG = -1e30

A_HEADS = 16
A_HEAD_DIM = 64
A_KV_RANK = 256
A_IDX_HEADS = 8
A_IDX_DIM = 64
A_TOPK_MAX = 256
A_QBLOCK = 128
A_KTILE = 256

B_K_HEADS = 8
B_V_HEADS = 16
B_HEAD_DIM = 128
B_CONV = 4
B_SUPER = 256

P_HEADS = 8
P_N_KEYS = 128
P_HALF = 128
P_TOPK = 16

INT_MIN = int(jnp.iinfo(jnp.int32).min)

_NT = (((1,), (1,)), ((), ()))
_TN = (((0,), (0,)), ((), ()))


def _vmem_limit(mib):
    return pltpu.CompilerParams(vmem_limit_bytes=mib * 1024 * 1024)


def _params(sem, mib=48):
    return pltpu.CompilerParams(dimension_semantics=sem, vmem_limit_bytes=mib * 1024 * 1024)


def _dot(a, b):
    return jnp.dot(a, b, preferred_element_type=F32)


def _dot_nt(a, b):
    return lax.dot_general(a, b, _NT, preferred_element_type=F32)


def _split_bf16(a):
    hi = a.astype(BF16)
    lo = (a - hi.astype(F32)).astype(BF16)
    return hi, lo


def _dot3(a, b):
    ah, al = _split_bf16(a)
    bh, bl = _split_bf16(b)
    return _dot(ah, bh) + (_dot(ah, bl) + _dot(al, bh))


def _rms(x):
    return x * lax.rsqrt(jnp.mean(x * x, axis=-1, keepdims=True) + RMS_EPS)


def _adaln_kernel(c_ref, w_ref, b_ref, o_ref):
    c = c_ref[...]
    cond = c * jax.nn.sigmoid(c)
    o_ref[0] = jnp.dot(cond, w_ref[0], preferred_element_type=F32, precision=HIGHEST) + b_ref[0]


def _adaln(c, ada_w, ada_b):
    depth, d, d6 = ada_w.shape
    b = c.shape[0]
    tn = 1536
    return pl.pallas_call(
        _adaln_kernel,
        grid=(depth, d6 // tn),
        in_specs=[pl.BlockSpec((b, d), lambda l, j: (0, 0)),
                  pl.BlockSpec((1, d, tn), lambda l, j: (l, 0, j)),
                  pl.BlockSpec((1, 1, tn), lambda l, j: (l, 0, j))],
        out_specs=pl.BlockSpec((1, b, tn), lambda l, j: (l, 0, j)),
        out_shape=jax.ShapeDtypeStruct((depth, b, d6), F32),
        compiler_params=_params(("arbitrary", "arbitrary")),
        name="adaln",
    )(c, ada_w, ada_b.reshape(depth, 1, d6))


def _nmm_kernel(x_ref, sh_ref, sc_ref, w_ref, o_ref, h_ref):
    @pl.when(pl.program_id(1) == 0)
    def _():
        h = _rms(x_ref[...]) * (1.0 + sc_ref[0]) + sh_ref[0]
        h_ref[...] = h.astype(BF16)

    o_ref[...] = _dot(h_ref[...], w_ref[...]).astype(o_ref.dtype)


def _norm_mod_matmul(x2, shift, scale, w, out_dtype, seq, tm, tn):
    n, d = x2.shape
    n_out = w.shape[1]
    per_b = seq // tm
    return pl.pallas_call(
        _nmm_kernel,
        grid=(n // tm, n_out // tn),
        in_specs=[pl.BlockSpec((tm, d), lambda i, j: (i, 0)),
                  pl.BlockSpec((1, 1, d), lambda i, j: (i // per_b, 0, 0)),
                  pl.BlockSpec((1, 1, d), lambda i, j: (i // per_b, 0, 0)),
                  pl.BlockSpec((d, tn), lambda i, j: (0, j))],
        out_specs=pl.BlockSpec((tm, tn), lambda i, j: (i, j)),
        out_shape=jax.ShapeDtypeStruct((n, n_out), out_dtype),
        scratch_shapes=[pltpu.VMEM((tm, d), BF16)],
        compiler_params=_params(("arbitrary", "arbitrary")),
        name="norm_mod_matmul",
    )(x2, shift, scale, w)


def _mmres_kernel(a_ref, w_ref, x_ref, g_ref, o_ref):
    o_ref[...] = x_ref[...] + g_ref[0] * _dot(a_ref[...], w_ref[...])


def _matmul_residual(a, w, x2, gate, seq, tm):
    n, k = a.shape
    d = w.shape[1]
    per_b = seq // tm
    return pl.pallas_call(
        _mmres_kernel,
        grid=(n // tm,),
        in_specs=[pl.BlockSpec((tm, k), lambda i: (i, 0)),
                  pl.BlockSpec((k, d), lambda i: (0, 0)),
                  pl.BlockSpec((tm, d), lambda i: (i, 0)),
                  pl.BlockSpec((1, 1, d), lambda i: (i // per_b, 0, 0))],
        out_specs=pl.BlockSpec((tm, d), lambda i: (i, 0)),
        out_shape=jax.ShapeDtypeStruct((n, d), F32),
        compiler_params=_params(("arbitrary",)),
        name="matmul_residual",
    )(a, w, x2, gate)


def _dsa_kernel(q_ref, qidx_ref, kwq_ref, ckv_ref, kw_ref, gain_ref, wuk_ref, wuv_ref, wout_ref,
                x_ref, gm_ref, o_ref,
                ckvn_ref, keys_ref, ql_ref, p_ref, m_ref, l_ref, acc_ref, *, n_sel):
    qi = pl.program_id(1)
    tq = A_QBLOCK
    n_kt = qi + 1
    seq = keys_ref.shape[1]

    @pl.when(qi == 0)
    def _():
        ckvn_ref[...] = (_rms(ckv_ref[...].astype(F32)) * gain_ref[...]).astype(BF16)

    row = lax.broadcasted_iota(I32, (tq, LANES), 0)
    lane = lax.broadcasted_iota(I32, (tq, LANES), 1)
    qh = [qidx_ref[:, h * A_IDX_DIM:(h + 1) * A_IDX_DIM] for h in range(A_IDX_HEADS)]
    wq = kwq_ref[:, A_IDX_DIM:A_IDX_DIM + A_IDX_HEADS].astype(F32) * ((A_IDX_HEADS ** -0.5) * (A_IDX_DIM ** -0.5))
    wb = [jnp.broadcast_to(wq[:, h:h + 1], (tq, LANES)) for h in range(A_IDX_HEADS)]
    diag_inadm = (row < CHUNK) & (lane >= CHUNK)

    def score_tile(kt, carry):
        off = pl.multiple_of(kt * LANES, LANES)
        kt_k = kw_ref[pl.ds(off, LANES), 0:A_IDX_DIM]
        sc = jnp.zeros((tq, LANES), F32)
        for h in range(A_IDX_HEADS):
            sc = sc + wb[h] * jnp.maximum(_dot_nt(qh[h], kt_k), 0.0)
        bits = lax.bitcast_convert_type(sc, I32)
        key = jnp.where(bits < 0, bits ^ jnp.int32(0x7FFFFFFF), bits)
        key = jnp.where(diag_inadm & (kt == qi), jnp.int32(INT_MIN), key)
        keys_ref[:, pl.ds(off, LANES)] = key
        return carry

    lax.fori_loop(0, n_kt, score_tile, 0)

    @pl.when(n_kt * LANES < seq)
    def _():
        keys_ref[:, pl.ds(pl.multiple_of(n_kt * LANES, LANES), LANES)] = jnp.full((tq, LANES), INT_MIN, I32)

    def count(pred):
        def body(kt, acc):
            off = pl.multiple_of(kt * LANES, LANES)
            return acc + jnp.where(pred(keys_ref[:, pl.ds(off, LANES)], off), 1.0, 0.0)
        acc = lax.fori_loop(0, n_kt, body, jnp.zeros((tq, LANES), F32))
        return jnp.sum(acc, axis=1, keepdims=True)

    def count_ge(cand):
        cb = jnp.broadcast_to(cand, (tq, LANES))
        return count(lambda k, off: k >= cb)

    kf = float(n_sel)
    zero = jnp.zeros((tq, 1), I32)
    ans = jnp.where(count_ge(zero) >= kf, zero, jnp.int32(INT_MIN))

    def bit_body(b, ans):
        cand = ans + jnp.left_shift(jnp.int32(1), 30 - b)
        return jnp.where(count_ge(cand) >= kf, cand, ans)

    ans = lax.fori_loop(0, 31, bit_body, ans)
    thr = jnp.maximum(ans, jnp.int32(INT_MIN + 1))
    thr_b = jnp.broadcast_to(thr, (tq, LANES))
    n_gt = count(lambda k, off: k > thr_b)
    n_ge = count(lambda k, off: k >= thr_b)
    need = kf - n_gt

    m_ref[0:tq, :] = jnp.full((tq, 1), float(2 * seq), F32)

    @pl.when(jnp.max(n_ge) > kf)
    def _():
        def eq_before(bound):
            bb = jnp.broadcast_to(bound, (tq, LANES))
            return count(lambda k, off: (k == thr_b) & ((off + lane).astype(F32) < bb))

        nbits = max(1, (seq - 1).bit_length())

        def jb(b, bound):
            cand = bound + jnp.left_shift(jnp.int32(1), nbits - b).astype(F32)
            return jnp.where(eq_before(cand) <= need, cand, bound)

        m_ref[0:tq, :] = lax.fori_loop(0, nbits + 1, jb, jnp.zeros((tq, 1), F32))

    bound_b = jnp.broadcast_to(m_ref[0:tq, :], (tq, A_KTILE))
    thr_k = jnp.broadcast_to(thr, (tq, A_KTILE))
    lane_k = lax.broadcasted_iota(I32, (tq, A_KTILE), 1)

    for h in range(A_HEADS):
        qlat = _dot(q_ref[:, h * A_HEAD_DIM:(h + 1) * A_HEAD_DIM], wuk_ref[h]) * (A_HEAD_DIM ** -0.5)
        ql_ref[h * tq:(h + 1) * tq, :] = qlat.astype(BF16)

    m_ref[...] = jnp.full(m_ref.shape, NEG_BIG, F32)
    l_ref[...] = jnp.zeros(l_ref.shape, F32)
    acc_ref[...] = jnp.zeros(acc_ref.shape, F32)

    def att_tile(t, carry):
        off = pl.multiple_of(t * A_KTILE, A_KTILE)
        kv = ckvn_ref[pl.ds(off, A_KTILE), :]
        key = keys_ref[:, pl.ds(off, A_KTILE)]
        sel = (key > thr_k) | ((key == thr_k) & ((off + lane_k).astype(F32) < bound_b))
        s = _dot_nt(ql_ref[...], kv)
        for h in range(A_HEADS):
            rs = slice(h * tq, (h + 1) * tq)
            sh = jnp.where(sel, s[rs], NEG_BIG)
            m_old = m_ref[rs, :]
            m_new = jnp.maximum(m_old, jnp.max(sh, axis=1, keepdims=True))
            p = jnp.exp(sh - m_new)
            alpha = jnp.exp(m_old - m_new)
            l_ref[rs, :] = alpha * l_ref[rs, :] + jnp.sum(p, axis=1, keepdims=True)
            m_ref[rs, :] = m_new
            p_ref[rs, :] = p.astype(BF16)
            acc_ref[rs, :] = alpha * acc_ref[rs, :]
        acc_ref[...] += _dot(p_ref[...], kv)
        return carry

    lax.fori_loop(0, (n_kt + 1) // 2, att_tile, 0)

    outs = []
    for h in range(A_HEADS):
        rs = slice(h * tq, (h + 1) * tq)
        o_lat = acc_ref[rs, :] / l_ref[rs, :]
        outs.append(_dot(o_lat.astype(BF16), wuv_ref[h]))
    att = jnp.concatenate(outs, axis=1)
    o_ref[...] = x_ref[...] + gm_ref[0] * _dot(att.astype(BF16), wout_ref[...])


def _dsa(proj, x2, gate, kv_gain, wuk_t, wuv, wout, batch, seq):
    n, d = x2.shape
    nq = seq // A_QBLOCK
    hr = A_HEADS * A_HEAD_DIM
    n_sel = min(A_TOPK_MAX, seq // 4)
    return pl.pallas_call(
        functools.partial(_dsa_kernel, n_sel=n_sel),
        grid=(batch, nq),
        in_specs=[pl.BlockSpec((A_QBLOCK, hr), lambda b, i: (b * nq + i, 0)),
                  pl.BlockSpec((A_QBLOCK, 512), lambda b, i: (b * nq + i, 2)),
                  pl.BlockSpec((A_QBLOCK, LANES), lambda b, i: (b * nq + i, 14)),
                  pl.BlockSpec((seq, A_KV_RANK), lambda b, i: (b, 6)),
                  pl.BlockSpec((seq, LANES), lambda b, i: (b, 14)),
                  pl.BlockSpec((1, A_KV_RANK), lambda b, i: (0, 0)),
                  pl.BlockSpec((A_HEADS, A_HEAD_DIM, A_KV_RANK), lambda b, i: (0, 0, 0)),
                  pl.BlockSpec((A_HEADS, A_KV_RANK, A_HEAD_DIM), lambda b, i: (0, 0, 0)),
                  pl.BlockSpec((hr, d), lambda b, i: (0, 0)),
                  pl.BlockSpec((A_QBLOCK, d), lambda b, i: (b * nq + i, 0)),
                  pl.BlockSpec((1, 1, d), lambda b, i: (b, 0, 0))],
        out_specs=pl.BlockSpec((A_QBLOCK, d), lambda b, i: (b * nq + i, 0)),
        out_shape=jax.ShapeDtypeStruct((n, d), F32),
        scratch_shapes=[pltpu.VMEM((seq, A_KV_RANK), BF16),
                        pltpu.VMEM((A_QBLOCK, seq), I32),
                        pltpu.VMEM((A_HEADS * A_QBLOCK, A_KV_RANK), BF16),
                        pltpu.VMEM((A_HEADS * A_QBLOCK, A_KTILE), BF16),
                        pltpu.VMEM((A_HEADS * A_QBLOCK, 1), F32),
                        pltpu.VMEM((A_HEADS * A_QBLOCK, 1), F32),
                        pltpu.VMEM((A_HEADS * A_QBLOCK, A_KV_RANK), F32)],
        compiler_params=_params(("arbitrary", "arbitrary")),
        name="dsa_attention",
    )(proj, proj, proj, proj, proj, kv_gain, wuk_t, wuv, wout, x2, gate)


def _gdn_gate_kernel(g_ref, alog_ref, dtb_ref, nat_ref, tr_ref):
    g = g_ref[...]
    col = lax.broadcasted_iota(I32, g.shape, 1)
    beta = jax.nn.sigmoid(g)
    z = g + dtb_ref[...]
    softplus = jnp.maximum(z, 0.0) + jnp.log1p(jnp.exp(-jnp.abs(z)))
    gd = -jnp.exp(alog_ref[...]) * softplus
    r = lax.broadcasted_iota(I32, (B_SUPER, B_SUPER), 0)
    c = lax.broadcasted_iota(I32, (B_SUPER, B_SUPER), 1)
    same = (r >> 6) == (c >> 6)
    tril = jnp.where(same & (c <= r), 1.0, 0.0).astype(F32)
    gc = jnp.dot(tril, gd, preferred_element_type=F32, precision=HIGHEST)
    nat = jnp.where(col < B_V_HEADS, beta, gc)
    nat_ref[...] = nat
    tr_ref[0, 0] = nat.T


def _gdn_gates(gates, a_log, dt_bias, batch, seq):
    n = gates.shape[0]
    ns = seq // B_SUPER
    pad = lambda v: jnp.zeros((1, LANES), F32).at[0, B_V_HEADS:2 * B_V_HEADS].set(v.astype(F32))
    return pl.pallas_call(
        _gdn_gate_kernel,
        grid=(batch, ns),
        in_specs=[pl.BlockSpec((B_SUPER, LANES), lambda b, s: (b * ns + s, 0)),
                  pl.BlockSpec((1, LANES), lambda b, s: (0, 0)),
                  pl.BlockSpec((1, LANES), lambda b, s: (0, 0))],
        out_specs=[pl.BlockSpec((B_SUPER, LANES), lambda b, s: (b * ns + s, 0)),
                   pl.BlockSpec((1, 1, LANES, B_SUPER), lambda b, s: (b, s, 0, 0))],
        out_shape=[jax.ShapeDtypeStruct((n, LANES), F32),
                   jax.ShapeDtypeStruct((batch, ns, LANES, B_SUPER), F32)],
        compiler_params=_params(("arbitrary", "arbitrary")),
        name="gdn_gates",
    )(gates, pad(a_log), pad(dt_bias))


def _conv_silu(raw_ref, w_ref, pad_ref):
    t = raw_ref.shape[0]
    pad_ref[0:8, :] = jnp.zeros((8, LANES), F32)
    pad_ref[8:8 + t, :] = raw_ref[...].astype(F32)
    w = w_ref[...]
    y = w[3:4] * pad_ref[8:8 + t, :]
    for j in range(B_CONV - 1):
        y = y + w[j:j + 1] * pad_ref[5 + j:5 + j + t, :]
    return y * jax.nn.sigmoid(y)


def _l2n(y):
    return y * lax.rsqrt(jnp.sum(y * y, axis=-1, keepdims=True) + RMS_EPS)


def _gdn_kernel(q_ref, k_ref, v0_ref, v1_ref, z_ref, wq_ref, wk_ref, wv0_ref, wv1_ref,
                gnat_ref, gtr_ref, gain_ref, o_ref,
                pad_ref, qn_ref, kn_ref, vs_ref, u_ref, w_ref, qg_ref, kg_ref, at_ref, el_ref, oo_ref):
    hk = pl.program_id(1)
    seq = q_ref.shape[0]
    ns = seq // B_SUPER
    nc = seq // CHUNK

    qn_ref[...] = _l2n(_conv_silu(q_ref, wq_ref, pad_ref)) * (B_HEAD_DIM ** -0.5)
    kn_ref[...] = _l2n(_conv_silu(k_ref, wk_ref, pad_ref))
    vs_ref[0] = _conv_silu(v0_ref, wv0_ref, pad_ref)
    vs_ref[1] = _conv_silu(v1_ref, wv1_ref, pad_ref)

    r = lax.broadcasted_iota(I32, (B_SUPER, B_SUPER), 0)
    c = lax.broadcasted_iota(I32, (B_SUPER, B_SUPER), 1)
    same = (r >> 6) == (c >> 6)
    tril = same & (c <= r)
    strict = same & (c < r)
    is_last = c == (r | (CHUNK - 1))
    eye = jnp.where(r == c, 1.0, 0.0).astype(F32)
    lane = lax.broadcasted_iota(I32, (B_SUPER, LANES), 1)

    def precompute(s, carry):
        rows = pl.ds(pl.multiple_of(s * B_SUPER, B_SUPER), B_SUPER)
        kn = kn_ref[rows, :]
        qn = qn_ref[rows, :]
        knb = kn.astype(BF16)
        kk = _dot_nt(knb, knb)
        qk = _dot_nt(qn.astype(BF16), knb)
        gnat = gnat_ref[rows, :]
        for e in range(2):
            hv = 2 * hk + e
            bcol = jnp.sum(jnp.where(lane == hv, gnat, 0.0), axis=1, keepdims=True)
            gcol = jnp.sum(jnp.where(lane == hv + B_V_HEADS, gnat, 0.0), axis=1, keepdims=True)
            grow = gtr_ref[0, s, pl.ds(hv + B_V_HEADS, 1), :]
            grow_b = jnp.broadcast_to(grow, (B_SUPER, B_SUPER))
            dec = jnp.where(tril, jnp.exp(gcol - grow_b), 0.0)
            a = jnp.where(strict, bcol * kk * dec, 0.0)
            pw = -a
            tm = eye + pw
            for _ in range(5):
                pw = _dot3(pw, pw)
                tm = tm + _dot3(tm, pw)
            eg = jnp.exp(gcol)
            glast = jnp.sum(jnp.where(is_last, grow_b, 0.0), axis=1, keepdims=True)
            u_ref[e, rows, :] = _dot3(tm, vs_ref[e, rows, :] * bcol)
            w_ref[e, rows, :] = _dot3(tm, kn * (bcol * eg))
            qg_ref[e, rows, :] = qn * eg
            kg_ref[e, rows, :] = kn * jnp.exp(glast - gcol)
            el_ref[e, rows, :] = jnp.exp(glast)
            attn = jnp.where(tril, qk * dec, 0.0)
            for j in range(B_SUPER // CHUNK):
                blk = attn[j * CHUNK:(j + 1) * CHUNK, j * CHUNK:(j + 1) * CHUNK]
                at_ref[e, pl.ds(pl.multiple_of(s * B_SUPER + j * CHUNK, CHUNK), CHUNK), :] = blk
        return carry

    lax.fori_loop(0, ns, precompute, 0)

    def step(n, states):
        rows = pl.ds(pl.multiple_of(n * CHUNK, CHUNK), CHUNK)
        new = []
        for e in range(2):
            st = states[e]
            sb = st.astype(BF16)
            v_new = u_ref[e, rows, :] - _dot(w_ref[e, rows, :].astype(BF16), sb)
            vb = v_new.astype(BF16)
            o = _dot(qg_ref[e, rows, :].astype(BF16), sb) + _dot(at_ref[e, rows, :].astype(BF16), vb)
            oo_ref[e, rows, :] = o
            el = el_ref[e, pl.ds(n * CHUNK, 1), :]
            new.append(st * el + lax.dot_general(kg_ref[e, rows, :].astype(BF16), vb, _TN,
                                                 preferred_element_type=F32))
        return tuple(new)

    zero = jnp.zeros((B_HEAD_DIM, B_HEAD_DIM), F32)
    lax.fori_loop(0, nc, step, (zero, zero))

    for e in range(2):
        z = z_ref[:, e * B_HEAD_DIM:(e + 1) * B_HEAD_DIM].astype(F32)
        y = _rms(oo_ref[e]) * gain_ref[...] * (z * jax.nn.sigmoid(z))
        o_ref[:, e * B_HEAD_DIM:(e + 1) * B_HEAD_DIM] = y.astype(o_ref.dtype)


def _gdn(qkvz, conv_w, gnat, gtr, out_gain, batch, seq):
    n = qkvz.shape[0]
    hd = B_HEAD_DIM
    kh = B_K_HEADS
    ns = seq // B_SUPER
    col = lambda f: (lambda b, h: (b, f(h)))
    wcol = lambda f: (lambda b, h: (0, f(h)))
    return pl.pallas_call(
        _gdn_kernel,
        grid=(batch, kh),
        in_specs=[pl.BlockSpec((seq, hd), col(lambda h: h)),
                  pl.BlockSpec((seq, hd), col(lambda h: kh + h)),
                  pl.BlockSpec((seq, hd), col(lambda h: 2 * kh + 2 * h)),
                  pl.BlockSpec((seq, hd), col(lambda h: 2 * kh + 2 * h + 1)),
                  pl.BlockSpec((seq, 2 * hd), col(lambda h: 2 * kh + h)),
                  pl.BlockSpec((B_CONV, hd), wcol(lambda h: h)),
                  pl.BlockSpec((B_CONV, hd), wcol(lambda h: kh + h)),
                  pl.BlockSpec((B_CONV, hd), wcol(lambda h: 2 * kh + 2 * h)),
                  pl.BlockSpec((B_CONV, hd), wcol(lambda h: 2 * kh + 2 * h + 1)),
                  pl.BlockSpec((seq, LANES), lambda b, h: (b, 0)),
                  pl.BlockSpec((1, ns, LANES, B_SUPER), lambda b, h: (b, 0, 0, 0)),
                  pl.BlockSpec((1, hd), lambda b, h: (0, 0))],
        out_specs=pl.BlockSpec((seq, 2 * hd), lambda b, h: (b, h)),
        out_shape=jax.ShapeDtypeStruct((n, B_V_HEADS * hd), BF16),
        scratch_shapes=[pltpu.VMEM((seq + 8, hd), F32),
                        pltpu.VMEM((seq, hd), F32),
                        pltpu.VMEM((seq, hd), F32),
                        pltpu.VMEM((2, seq, hd), F32),
                        pltpu.VMEM((2, seq, hd), F32),
                        pltpu.VMEM((2, seq, hd), F32),
                        pltpu.VMEM((2, seq, hd), F32),
                        pltpu.VMEM((2, seq, hd), F32),
                        pltpu.VMEM((2, seq, CHUNK), F32),
                        pltpu.VMEM((2, seq, 1), F32),
                        pltpu.VMEM((2, seq, hd), F32)],
        compiler_params=_params(("arbitrary", "arbitrary")),
        name="gated_deltanet",
    )(qkvz, qkvz, qkvz, qkvz, qkvz, conv_w, conv_w, conv_w, conv_w, gnat, gtr, out_gain)


_CAND_W = [P_TOPK // (r + 1) for r in range(P_TOPK)]
_CAND_N = sum(_CAND_W)
_CAND_PAD = -(-_CAND_N // 8) * 8


def _top16(s, iota_k):
    vals, idxs = [], []
    for _ in range(P_TOPK):
        m = jnp.max(s, axis=0, keepdims=True)
        idx = jnp.min(jnp.where(s == m, iota_k, P_N_KEYS), axis=0, keepdims=True)
        vals.append(m)
        idxs.append(idx)
        s = jnp.where(iota_k == idx, -jnp.inf, s)
    return vals, idxs


def _peer_route_kernel(x_ref, sh_ref, sc_ref, wq_ref, sk_ref, ht_ref, rank_ref, cnt_ref, a_ref, b_ref):
    tn = x_ref.shape[0]
    h = _rms(x_ref[...]) * (1.0 + sc_ref[0]) + sh_ref[0]
    ht_ref[...] = h.T.astype(BF16)
    qb = _dot(h.astype(BF16), wq_ref[...]).astype(BF16)
    iota_k = lax.broadcasted_iota(I32, (P_N_KEYS, tn), 0)
    iota_c = lax.broadcasted_iota(I32, (_CAND_PAD, tn), 0)
    neg_pad = jnp.full((_CAND_PAD - _CAND_N, tn), -jnp.inf, F32)
    for hd in range(P_HEADS):
        base = hd * 2 * P_HALF
        s1 = _dot_nt(sk_ref[hd, 0], qb[:, base:base + P_HALF])
        s2 = _dot_nt(sk_ref[hd, 1], qb[:, base + P_HALF:base + 2 * P_HALF])
        v1, i1 = _top16(s1, iota_k)
        v2, i2 = _top16(s2, iota_k)
        v2s = jnp.concatenate(v2, axis=0)
        ea = [jnp.exp(v1[r] - v1[0]) for r in range(P_TOPK)]
        eb = jnp.exp(v2s - v2[0])
        cand = jnp.concatenate([v1[r] + v2s[0:_CAND_W[r]] for r in range(P_TOPK)] + [neg_pad], axis=0)
        gate = jnp.concatenate([ea[r] * eb[0:_CAND_W[r]] for r in range(P_TOPK)] + [jnp.zeros_like(neg_pad)],
                               axis=0)
        sel = jnp.zeros((_CAND_PAD, tn), F32)
        work = cand
        for _ in range(P_TOPK):
            m = jnp.max(work, axis=0, keepdims=True)
            idx = jnp.min(jnp.where(work == m, iota_c, _CAND_PAD), axis=0, keepdims=True)
            hit = iota_c == idx
            sel = jnp.where(hit, 1.0, sel)
            work = jnp.where(hit, -jnp.inf, work)
        zsum = jnp.sum(sel * gate, axis=0, keepdims=True)
        inv_z = 1.0 / zsum
        a_t = jnp.zeros((P_N_KEYS, tn), F32)
        cnt_t = jnp.zeros((P_N_KEYS, tn), F32)
        off = 0
        for r in range(P_TOPK):
            n_r = jnp.sum(sel[off:off + _CAND_W[r]], axis=0, keepdims=True)
            off += _CAND_W[r]
            hit = iota_k == i1[r]
            a_t = jnp.where(hit, ea[r], a_t)
            cnt_t = jnp.where(hit, n_r, cnt_t)
        rank_t = jnp.full((P_N_KEYS, tn), float(P_TOPK), F32)
        b_t = jnp.zeros((P_N_KEYS, tn), F32)
        for r in range(P_TOPK):
            hit = iota_k == i2[r]
            rank_t = jnp.where(hit, float(r), rank_t)
            b_t = jnp.where(hit, eb[r:r + 1] * inv_z, b_t)
        rank_ref[hd] = rank_t
        cnt_ref[hd] = cnt_t
        a_ref[hd] = a_t
        b_ref[hd] = b_t


def _peer_route(x2, shift, scale, wq, sub_keys, seq, tn):
    n, d = x2.shape
    per_b = seq // tn
    tok = lambda i: (0, 0, i)
    arr = jax.ShapeDtypeStruct((P_HEADS, P_N_KEYS, n), F32)
    return pl.pallas_call(
        _peer_route_kernel,
        grid=(n // tn,),
        in_specs=[pl.BlockSpec((tn, d), lambda i: (i, 0)),
                  pl.BlockSpec((1, 1, d), lambda i: (i // per_b, 0, 0)),
                  pl.BlockSpec((1, 1, d), lambda i: (i // per_b, 0, 0)),
                  pl.BlockSpec(wq.shape, lambda i: (0, 0)),
                  pl.BlockSpec(sub_keys.shape, lambda i: (0, 0, 0, 0))],
        out_specs=[pl.BlockSpec((d, tn), lambda i: (0, i)),
                   pl.BlockSpec((P_HEADS, P_N_KEYS, tn), tok),
                   pl.BlockSpec((P_HEADS, P_N_KEYS, tn), tok),
                   pl.BlockSpec((P_HEADS, P_N_KEYS, tn), tok),
                   pl.BlockSpec((P_HEADS, P_N_KEYS, tn), tok)],
        out_shape=[jax.ShapeDtypeStruct((d, n), BF16), arr, arr, arr, arr],
        compiler_params=_params(("arbitrary",)),
        name="peer_route",
    )(x2, shift, scale, wq, sub_keys)


def _gelu(x):
    return 0.5 * x * (1.0 + lax.erf(x * (2.0 ** -0.5)))


def _peer_dense_kernel(ht_ref, u_ref, vt_ref, rank_ref, cnt_ref, a_ref, b_ref, x_ref, gf_ref, fn_ref,
                       o_ref, acc_ref, *, final):
    e = pl.program_id(1)
    eb = u_ref.shape[0]
    n_i = eb // P_N_KEYS

    @pl.when(e == 0)
    def _():
        acc_ref[...] = jnp.zeros(acc_ref.shape, F32)

    act = _dot(u_ref[...], ht_ref[...])
    parts = []
    for ii in range(n_i):
        i = e * n_i + ii
        g = jnp.zeros((P_N_KEYS, act.shape[1]), F32)
        for hd in range(P_HEADS):
            cnt = cnt_ref[hd, pl.ds(i, 1), :]
            a = a_ref[hd, pl.ds(i, 1), :]
            g = g + jnp.where(rank_ref[hd] < cnt, a * b_ref[hd], 0.0)
        parts.append((g * _gelu(act[ii * P_N_KEYS:(ii + 1) * P_N_KEYS])).astype(BF16))
    p = jnp.concatenate(parts, axis=0)
    acc_ref[...] += _dot(vt_ref[...], p)

    @pl.when(e == pl.num_programs(1) - 1)
    def _():
        xn = x_ref[...] + gf_ref[0] * acc_ref[...].T
        if final:
            xn = _rms(xn) * fn_ref[...]
        o_ref[...] = xn


def _peer_dense(ht, u, vt, rank, cnt, a, b, x2, gate, fgain, seq, tn, eb, final):
    n, d = x2.shape
    n_e = u.shape[0]
    per_b = seq // tn
    tok = lambda t, e: (0, 0, t)
    return pl.pallas_call(
        functools.partial(_peer_dense_kernel, final=final),
        grid=(n // tn, n_e // eb),
        in_specs=[pl.BlockSpec((d, tn), lambda t, e: (0, t)),
                  pl.BlockSpec((eb, d), lambda t, e: (e, 0)),
                  pl.BlockSpec((d, eb), lambda t, e: (0, e)),
                  pl.BlockSpec((P_HEADS, P_N_KEYS, tn), tok),
                  pl.BlockSpec((P_HEADS, P_N_KEYS, tn), tok),
                  pl.BlockSpec((P_HEADS, P_N_KEYS, tn), tok),
                  pl.BlockSpec((P_HEADS, P_N_KEYS, tn), tok),
                  pl.BlockSpec((tn, d), lambda t, e: (t, 0)),
                  pl.BlockSpec((1, 1, d), lambda t, e: (t // per_b, 0, 0)),
                  pl.BlockSpec((1, d), lambda t, e: (0, 0))],
        out_specs=pl.BlockSpec((tn, d), lambda t, e: (t, 0)),
        out_shape=jax.ShapeDtypeStruct((n, d), F32),
        scratch_shapes=[pltpu.VMEM((d, tn), F32)],
        compiler_params=_params(("arbitrary", "arbitrary")),
        name="peer_dense",
    )(ht, u, vt, rank, cnt, a, b, x2, gate, fgain)


def _peer(x2, shift, scale, gate, w_q, sub_keys, u_tab, v_tab, fgain, seq, final):
    tn = min(512, seq)
    ht, rank, cnt, a, b = _peer_route(x2, shift, scale, w_q.astype(BF16), sub_keys.astype(BF16), seq,
                                      min(256, seq))
    return _peer_dense(ht, u_tab.astype(BF16), v_tab.T.astype(BF16), rank, cnt, a, b, x2, gate, fgain,
                       seq, tn, 512, final)


def kernel(x, c, a_w_in, a_kv_norm, a_w_uk, a_w_uv, a_w_out, b_w_in, b_conv, b_a_log, b_dt_bias,
           b_out_norm, b_w_out, p_w_q, p_sub_keys, p_u, p_v, ada_w, ada_b, final_norm):
    batch, seq, d = x.shape
    n = batch * seq
    depth = ada_w.shape[0]
    x2 = x.reshape(n, d)
    mod = _adaln(c, ada_w, ada_b).reshape(depth, batch, 6, 1, d)
    fgain = final_norm.reshape(1, d)
    tm = min(512, seq)
    for layer in range(depth):
        sh_m, sc_m, g_m, sh_f, sc_f, g_f = (mod[layer, :, k] for k in range(6))
        j = layer // 2
        if layer % 2 == 0:
            w_in = a_w_in[j]
            hr = A_HEADS * A_HEAD_DIM
            c0, c1, c2 = hr + A_KV_RANK, hr + A_KV_RANK + 512, hr + A_KV_RANK + 512 + A_IDX_DIM + A_IDX_HEADS
            w_a = jnp.concatenate([w_in[:, :hr], w_in[:, c0:c1], w_in[:, hr:c0], w_in[:, c1:c2],
                                   jnp.zeros((d, 1920 - c2), w_in.dtype)], axis=1).astype(BF16)
            proj = _norm_mod_matmul(x2, sh_m, sc_m, w_a, BF16, seq, tm, 1920)
            x2 = _dsa(proj, x2, g_m, a_kv_norm[j].reshape(1, -1),
                      a_w_uk[j].transpose(1, 2, 0).astype(BF16), a_w_uv[j].transpose(1, 0, 2).astype(BF16),
                      a_w_out[j].astype(BF16), batch, seq)
        else:
            w_in = b_w_in[j]
            nqkvz = 2 * B_K_HEADS * B_HEAD_DIM + 2 * B_V_HEADS * B_HEAD_DIM
            w_g = jnp.concatenate([w_in[:, nqkvz:], jnp.zeros((d, LANES - 2 * B_V_HEADS), w_in.dtype)], axis=1)
            qkvz = _norm_mod_matmul(x2, sh_m, sc_m, w_in[:, :nqkvz].astype(BF16), BF16, seq, tm, 1024)
            gates = _norm_mod_matmul(x2, sh_m, sc_m, w_g.astype(BF16), F32, seq, tm, LANES)
            gnat, gtr = _gdn_gates(gates, b_a_log[j], b_dt_bias[j], batch, seq)
            onorm = _gdn(qkvz, b_conv[j].reshape(B_CONV, -1), gnat, gtr, b_out_norm[j].reshape(1, -1), batch, seq)
            x2 = _matmul_residual(onorm, b_w_out[j].astype(BF16), x2, g_m, seq, tm)
        x2 = _peer(x2, sh_f, sc_f, g_f, p_w_q[layer], p_sub_keys[layer], p_u[layer], p_v[layer], fgain, seq,
                   final=(layer == depth - 1))
    return x2.reshape(batch, seq, d)
```

```python
import functools

import jax
import jax.numpy as jnp
from jax import lax
from jax.experimental import pallas as pl
from jax.experimental.pallas import tpu as pltpu

F32 = jnp.float32
BF16 = jnp.bfloat16
I32 = jnp.int32
HIGHEST = lax.Precision.HIGHEST

RMS_EPS = 1e-6
CHUNK = 64
LANES = 128
SUBLANES = 8
NEG_BIG = -1e30

A_HEADS = 16
A_HEAD_DIM = 64
A_KV_RANK = 256
A_IDX_HEADS = 8
A_IDX_DIM = 64
A_TOPK_MAX = 256
A_QBLOCK = 128
A_KTILE = 256

B_K_HEADS = 8
B_V_HEADS = 16
B_HEAD_DIM = 128
B_CONV = 4
B_SUPER = 256

P_HEADS = 8
P_N_KEYS = 128
P_HALF = 128
P_TOPK = 16
P_TOKEN_TILE = 512
P_EXPERT_BLOCK = 1024
P_SUB_ROWS = 4
P_SLAB_GROUP = 4

INT_MIN = int(jnp.iinfo(jnp.int32).min)

_NT = (((1,), (1,)), ((), ()))
_TN = (((0,), (0,)), ((), ()))


def _vmem_limit(mib):
    return pltpu.CompilerParams(vmem_limit_bytes=mib * 1024 * 1024)


def _params(sem, mib=48):
    return pltpu.CompilerParams(dimension_semantics=sem, vmem_limit_bytes=mib * 1024 * 1024)


def _dot(a, b):
    return jnp.dot(a, b, preferred_element_type=F32)


def _dot_nt(a, b):
    return lax.dot_general(a, b, _NT, preferred_element_type=F32)


def _split_bf16(a):
    hi = a.astype(BF16)
    lo = (a - hi.astype(F32)).astype(BF16)
    return hi, lo


def _dot3(a, b):
    ah, al = _split_bf16(a)
    bh, bl = _split_bf16(b)
    return _dot(ah, bh) + (_dot(ah, bl) + _dot(al, bh))


def _rms(x):
    return x * lax.rsqrt(jnp.mean(x * x, axis=-1, keepdims=True) + RMS_EPS)


def _adaln_kernel(c_ref, w_ref, b_ref, o_ref):
    c = c_ref[...]
    cond = c * jax.nn.sigmoid(c)
    o_ref[0] = jnp.dot(cond, w_ref[0], preferred_element_type=F32, precision=HIGHEST) + b_ref[0]


def _adaln(c, ada_w, ada_b):
    depth, d, d6 = ada_w.shape
    b = c.shape[0]
    tn = 1536
    return pl.pallas_call(
        _adaln_kernel,
        grid=(depth, d6 // tn),
        in_specs=[pl.BlockSpec((b, d), lambda l, j: (0, 0)),
                  pl.BlockSpec((1, d, tn), lambda l, j: (l, 0, j)),
                  pl.BlockSpec((1, 1, tn), lambda l, j: (l, 0, j))],
        out_specs=pl.BlockSpec((1, b, tn), lambda l, j: (l, 0, j)),
        out_shape=jax.ShapeDtypeStruct((depth, b, d6), F32),
        compiler_params=_params(("arbitrary", "arbitrary")),
        name="adaln",
    )(c, ada_w, ada_b.reshape(depth, 1, d6))


def _nmm_kernel(x_ref, sh_ref, sc_ref, w_ref, o_ref, h_ref):
    @pl.when(pl.program_id(1) == 0)
    def _():
        h = _rms(x_ref[...]) * (1.0 + sc_ref[0]) + sh_ref[0]
        h_ref[...] = h.astype(BF16)

    o_ref[...] = _dot(h_ref[...], w_ref[...]).astype(o_ref.dtype)


def _norm_mod_matmul(x2, shift, scale, w, out_dtype, seq, tm, tn):
    n, d = x2.shape
    n_out = w.shape[1]
    per_b = seq // tm
    return pl.pallas_call(
        _nmm_kernel,
        grid=(n // tm, n_out // tn),
        in_specs=[pl.BlockSpec((tm, d), lambda i, j: (i, 0)),
                  pl.BlockSpec((1, 1, d), lambda i, j: (i // per_b, 0, 0)),
                  pl.BlockSpec((1, 1, d), lambda i, j: (i // per_b, 0, 0)),
                  pl.BlockSpec((d, tn), lambda i, j: (0, j))],
        out_specs=pl.BlockSpec((tm, tn), lambda i, j: (i, j)),
        out_shape=jax.ShapeDtypeStruct((n, n_out), out_dtype),
        scratch_shapes=[pltpu.VMEM((tm, d), BF16)],
        compiler_params=_params(("arbitrary", "arbitrary")),
        name="norm_mod_matmul",
    )(x2, shift, scale, w)


def _mmres_kernel(a_ref, w_ref, x_ref, g_ref, o_ref):
    o_ref[...] = x_ref[...] + g_ref[0] * _dot(a_ref[...], w_ref[...])


def _matmul_residual(a, w, x2, gate, seq, tm):
    n, k = a.shape
    d = w.shape[1]
    per_b = seq // tm
    return pl.pallas_call(
        _mmres_kernel,
        grid=(n // tm,),
        in_specs=[pl.BlockSpec((tm, k), lambda i: (i, 0)),
                  pl.BlockSpec((k, d), lambda i: (0, 0)),
                  pl.BlockSpec((tm, d), lambda i: (i, 0)),
                  pl.BlockSpec((1, 1, d), lambda i: (i // per_b, 0, 0))],
        out_specs=pl.BlockSpec((tm, d), lambda i: (i, 0)),
        out_shape=jax.ShapeDtypeStruct((n, d), F32),
        compiler_params=_params(("arbitrary",)),
        name="matmul_residual",
    )(a, w, x2, gate)


def _dsa_kernel(q_ref, qidx_ref, kwq_ref, ckv_ref, kw_ref, gain_ref, wuk_ref, wuv_ref, wout_ref,
                x_ref, gm_ref, o_ref,
                ckvn_ref, keys_ref, ql_ref, p_ref, m_ref, l_ref, acc_ref, *, n_sel):
    qi = pl.program_id(1)
    tq = A_QBLOCK
    n_kt = qi + 1
    seq = keys_ref.shape[1]

    @pl.when(qi == 0)
    def _():
        ckvn_ref[...] = (_rms(ckv_ref[...].astype(F32)) * gain_ref[...]).astype(BF16)

    row = lax.broadcasted_iota(I32, (tq, LANES), 0)
    lane = lax.broadcasted_iota(I32, (tq, LANES), 1)
    qh = [qidx_ref[:, h * A_IDX_DIM:(h + 1) * A_IDX_DIM] for h in range(A_IDX_HEADS)]
    wq = kwq_ref[:, A_IDX_DIM:A_IDX_DIM + A_IDX_HEADS].astype(F32) * ((A_IDX_HEADS ** -0.5) * (A_IDX_DIM ** -0.5))
    wb = [jnp.broadcast_to(wq[:, h:h + 1], (tq, LANES)) for h in range(A_IDX_HEADS)]
    diag_inadm = (row < CHUNK) & (lane >= CHUNK)

    def score_tile(kt, carry):
        off = pl.multiple_of(kt * LANES, LANES)
        kt_k = kw_ref[pl.ds(off, LANES), 0:A_IDX_DIM]
        sc = jnp.zeros((tq, LANES), F32)
        for h in range(A_IDX_HEADS):
            sc = sc + wb[h] * jnp.maximum(_dot_nt(qh[h], kt_k), 0.0)
        bits = lax.bitcast_convert_type(sc, I32)
        key = jnp.where(bits < 0, bits ^ jnp.int32(0x7FFFFFFF), bits)
        key = jnp.where(diag_inadm & (kt == qi), jnp.int32(INT_MIN), key)
        keys_ref[:, pl.ds(off, LANES)] = key
        return carry

    lax.fori_loop(0, n_kt, score_tile, 0)

    @pl.when(n_kt * LANES < seq)
    def _():
        keys_ref[:, pl.ds(pl.multiple_of(n_kt * LANES, LANES), LANES)] = jnp.full((tq, LANES), INT_MIN, I32)

    def count(pred):
        def body(kt, acc):
            off = pl.multiple_of(kt * LANES, LANES)
            return acc + jnp.where(pred(keys_ref[:, pl.ds(off, LANES)], off), 1.0, 0.0)
        acc = lax.fori_loop(0, n_kt, body, jnp.zeros((tq, LANES), F32))
        return jnp.sum(acc, axis=1, keepdims=True)

    def count_ge(cand):
        cb = jnp.broadcast_to(cand, (tq, LANES))
        return count(lambda k, off: k >= cb)

    kf = float(n_sel)
    zero = jnp.zeros((tq, 1), I32)
    ans = jnp.where(count_ge(zero) >= kf, zero, jnp.int32(INT_MIN))

    def bit_body(b, ans):
        cand = ans + jnp.left_shift(jnp.int32(1), 30 - b)
        return jnp.where(count_ge(cand) >= kf, cand, ans)

    ans = lax.fori_loop(0, 31, bit_body, ans)
    thr = jnp.maximum(ans, jnp.int32(INT_MIN + 1))
    thr_b = jnp.broadcast_to(thr, (tq, LANES))
    n_gt = count(lambda k, off: k > thr_b)
    n_ge = count(lambda k, off: k >= thr_b)
    need = kf - n_gt

    m_ref[0:tq, :] = jnp.full((tq, 1), float(2 * seq), F32)

    @pl.when(jnp.max(n_ge) > kf)
    def _():
        def eq_before(bound):
            bb = jnp.broadcast_to(bound, (tq, LANES))
            return count(lambda k, off: (k == thr_b) & ((off + lane).astype(F32) < bb))

        nbits = max(1, (seq - 1).bit_length())

        def jb(b, bound):
            cand = bound + jnp.left_shift(jnp.int32(1), nbits - b).astype(F32)
            return jnp.where(eq_before(cand) <= need, cand, bound)

        m_ref[0:tq, :] = lax.fori_loop(0, nbits + 1, jb, jnp.zeros((tq, 1), F32))

    bound_b = jnp.broadcast_to(m_ref[0:tq, :], (tq, A_KTILE))
    thr_k = jnp.broadcast_to(thr, (tq, A_KTILE))
    lane_k = lax.broadcasted_iota(I32, (tq, A_KTILE), 1)

    for h in range(A_HEADS):
        qlat = _dot(q_ref[:, h * A_HEAD_DIM:(h + 1) * A_HEAD_DIM], wuk_ref[h]) * (A_HEAD_DIM ** -0.5)
        ql_ref[h * tq:(h + 1) * tq, :] = qlat.astype(BF16)

    m_ref[...] = jnp.full(m_ref.shape, NEG_BIG, F32)
    l_ref[...] = jnp.zeros(l_ref.shape, F32)
    acc_ref[...] = jnp.zeros(acc_ref.shape, F32)

    def att_tile(t, carry):
        off = pl.multiple_of(t * A_KTILE, A_KTILE)
        kv = ckvn_ref[pl.ds(off, A_KTILE), :]
        key = keys_ref[:, pl.ds(off, A_KTILE)]
        sel = (key > thr_k) | ((key == thr_k) & ((off + lane_k).astype(F32) < bound_b))
        s = _dot_nt(ql_ref[...], kv)
        for h in range(A_HEADS):
            rs = slice(h * tq, (h + 1) * tq)
            sh = jnp.where(sel, s[rs], NEG_BIG)
            m_old = m_ref[rs, :]
            m_new = jnp.maximum(m_old, jnp.max(sh, axis=1, keepdims=True))
            p = jnp.exp(sh - m_new)
            alpha = jnp.exp(m_old - m_new)
            l_ref[rs, :] = alpha * l_ref[rs, :] + jnp.sum(p, axis=1, keepdims=True)
            m_ref[rs, :] = m_new
            p_ref[rs, :] = p.astype(BF16)
            acc_ref[rs, :] = alpha * acc_ref[rs, :]
        acc_ref[...] += _dot(p_ref[...], kv)
        return carry

    lax.fori_loop(0, (n_kt + 1) // 2, att_tile, 0)

    outs = []
    for h in range(A_HEADS):
        rs = slice(h * tq, (h + 1) * tq)
        o_lat = acc_ref[rs, :] / l_ref[rs, :]
        outs.append(_dot(o_lat.astype(BF16), wuv_ref[h]))
    att = jnp.concatenate(outs, axis=1)
    o_ref[...] = x_ref[...] + gm_ref[0] * _dot(att.astype(BF16), wout_ref[...])


def _dsa(proj, x2, gate, kv_gain, wuk_t, wuv, wout, batch, seq):
    n, d = x2.shape
    nq = seq // A_QBLOCK
    hr = A_HEADS * A_HEAD_DIM
    n_sel = min(A_TOPK_MAX, seq // 4)
    return pl.pallas_call(
        functools.partial(_dsa_kernel, n_sel=n_sel),
        grid=(batch, nq),
        in_specs=[pl.BlockSpec((A_QBLOCK, hr), lambda b, i: (b * nq + i, 0)),
                  pl.BlockSpec((A_QBLOCK, 512), lambda b, i: (b * nq + i, 2)),
                  pl.BlockSpec((A_QBLOCK, LANES), lambda b, i: (b * nq + i, 14)),
                  pl.BlockSpec((seq, A_KV_RANK), lambda b, i: (b, 6)),
                  pl.BlockSpec((seq, LANES), lambda b, i: (b, 14)),
                  pl.BlockSpec((1, A_KV_RANK), lambda b, i: (0, 0)),
                  pl.BlockSpec((A_HEADS, A_HEAD_DIM, A_KV_RANK), lambda b, i: (0, 0, 0)),
                  pl.BlockSpec((A_HEADS, A_KV_RANK, A_HEAD_DIM), lambda b, i: (0, 0, 0)),
                  pl.BlockSpec((hr, d), lambda b, i: (0, 0)),
                  pl.BlockSpec((A_QBLOCK, d), lambda b, i: (b * nq + i, 0)),
                  pl.BlockSpec((1, 1, d), lambda b, i: (b, 0, 0))],
        out_specs=pl.BlockSpec((A_QBLOCK, d), lambda b, i: (b * nq + i, 0)),
        out_shape=jax.ShapeDtypeStruct((n, d), F32),
        scratch_shapes=[pltpu.VMEM((seq, A_KV_RANK), BF16),
                        pltpu.VMEM((A_QBLOCK, seq), I32),
                        pltpu.VMEM((A_HEADS * A_QBLOCK, A_KV_RANK), BF16),
                        pltpu.VMEM((A_HEADS * A_QBLOCK, A_KTILE), BF16),
                        pltpu.VMEM((A_HEADS * A_QBLOCK, 1), F32),
                        pltpu.VMEM((A_HEADS * A_QBLOCK, 1), F32),
                        pltpu.VMEM((A_HEADS * A_QBLOCK, A_KV_RANK), F32)],
        compiler_params=_params(("arbitrary", "arbitrary")),
        name="dsa_attention",
    )(proj, proj, proj, proj, proj, kv_gain, wuk_t, wuv, wout, x2, gate)


def _gdn_gate_kernel(g_ref, alog_ref, dtb_ref, nat_ref, tr_ref):
    g = g_ref[...]
    col = lax.broadcasted_iota(I32, g.shape, 1)
    beta = jax.nn.sigmoid(g)
    z = g + dtb_ref[...]
    softplus = jnp.maximum(z, 0.0) + jnp.log1p(jnp.exp(-jnp.abs(z)))
    gd = -jnp.exp(alog_ref[...]) * softplus
    r = lax.broadcasted_iota(I32, (B_SUPER, B_SUPER), 0)
    c = lax.broadcasted_iota(I32, (B_SUPER, B_SUPER), 1)
    same = (r >> 6) == (c >> 6)
    tril = jnp.where(same & (c <= r), 1.0, 0.0).astype(F32)
    gc = jnp.dot(tril, gd, preferred_element_type=F32, precision=HIGHEST)
    nat = jnp.where(col < B_V_HEADS, beta, gc)
    nat_ref[...] = nat
    tr_ref[0, 0] = nat.T


def _gdn_gates(gates, a_log, dt_bias, batch, seq):
    n = gates.shape[0]
    ns = seq // B_SUPER
    pad = lambda v: jnp.zeros((1, LANES), F32).at[0, B_V_HEADS:2 * B_V_HEADS].set(v.astype(F32))
    return pl.pallas_call(
        _gdn_gate_kernel,
        grid=(batch, ns),
        in_specs=[pl.BlockSpec((B_SUPER, LANES), lambda b, s: (b * ns + s, 0)),
                  pl.BlockSpec((1, LANES), lambda b, s: (0, 0)),
                  pl.BlockSpec((1, LANES), lambda b, s: (0, 0))],
        out_specs=[pl.BlockSpec((B_SUPER, LANES), lambda b, s: (b * ns + s, 0)),
                   pl.BlockSpec((1, 1, LANES, B_SUPER), lambda b, s: (b, s, 0, 0))],
        out_shape=[jax.ShapeDtypeStruct((n, LANES), F32),
                   jax.ShapeDtypeStruct((batch, ns, LANES, B_SUPER), F32)],
        compiler_params=_params(("arbitrary", "arbitrary")),
        name="gdn_gates",
    )(gates, pad(a_log), pad(dt_bias))


def _conv_silu(raw_ref, w_ref, pad_ref):
    t = raw_ref.shape[0]
    pad_ref[0:8, :] = jnp.zeros((8, LANES), F32)
    pad_ref[8:8 + t, :] = raw_ref[...].astype(F32)
    w = w_ref[...]
    y = w[3:4] * pad_ref[8:8 + t, :]
    for j in range(B_CONV - 1):
        y = y + w[j:j + 1] * pad_ref[5 + j:5 + j + t, :]
    return y * jax.nn.sigmoid(y)


def _l2n(y):
    return y * lax.rsqrt(jnp.sum(y * y, axis=-1, keepdims=True) + RMS_EPS)


def _gdn_kernel(q_ref, k_ref, v0_ref, v1_ref, z_ref, wq_ref, wk_ref, wv0_ref, wv1_ref,
                gnat_ref, gtr_ref, gain_ref, o_ref,
                pad_ref, qn_ref, kn_ref, vs_ref, u_ref, w_ref, qg_ref, kg_ref, at_ref, el_ref, oo_ref):
    hk = pl.program_id(1)
    seq = q_ref.shape[0]
    ns = seq // B_SUPER
    nc = seq // CHUNK

    qn_ref[...] = _l2n(_conv_silu(q_ref, wq_ref, pad_ref)) * (B_HEAD_DIM ** -0.5)
    kn_ref[...] = _l2n(_conv_silu(k_ref, wk_ref, pad_ref))
    vs_ref[0] = _conv_silu(v0_ref, wv0_ref, pad_ref)
    vs_ref[1] = _conv_silu(v1_ref, wv1_ref, pad_ref)

    r = lax.broadcasted_iota(I32, (B_SUPER, B_SUPER), 0)
    c = lax.broadcasted_iota(I32, (B_SUPER, B_SUPER), 1)
    same = (r >> 6) == (c >> 6)
    tril = same & (c <= r)
    strict = same & (c < r)
    is_last = c == (r | (CHUNK - 1))
    eye = jnp.where(r == c, 1.0, 0.0).astype(F32)
    lane = lax.broadcasted_iota(I32, (B_SUPER, LANES), 1)

    def precompute(s, carry):
        rows = pl.ds(pl.multiple_of(s * B_SUPER, B_SUPER), B_SUPER)
        kn = kn_ref[rows, :]
        qn = qn_ref[rows, :]
        knb = kn.astype(BF16)
        kk = _dot_nt(knb, knb)
        qk = _dot_nt(qn.astype(BF16), knb)
        gnat = gnat_ref[rows, :]
        for e in range(2):
            hv = 2 * hk + e
            bcol = jnp.sum(jnp.where(lane == hv, gnat, 0.0), axis=1, keepdims=True)
            gcol = jnp.sum(jnp.where(lane == hv + B_V_HEADS, gnat, 0.0), axis=1, keepdims=True)
            grow = gtr_ref[0, s, pl.ds(hv + B_V_HEADS, 1), :]
            grow_b = jnp.broadcast_to(grow, (B_SUPER, B_SUPER))
            dec = jnp.where(tril, jnp.exp(gcol - grow_b), 0.0)
            a = jnp.where(strict, bcol * kk * dec, 0.0)
            pw = -a
            tm = eye + pw
            for _ in range(5):
                pw = _dot3(pw, pw)
                tm = tm + _dot3(tm, pw)
            eg = jnp.exp(gcol)
            glast = jnp.sum(jnp.where(is_last, grow_b, 0.0), axis=1, keepdims=True)
            u_ref[e, rows, :] = _dot3(tm, vs_ref[e, rows, :] * bcol)
            w_ref[e, rows, :] = _dot3(tm, kn * (bcol * eg))
            qg_ref[e, rows, :] = qn * eg
            kg_ref[e, rows, :] = kn * jnp.exp(glast - gcol)
            el_ref[e, rows, :] = jnp.exp(glast)
            attn = jnp.where(tril, qk * dec, 0.0)
            for j in range(B_SUPER // CHUNK):
                blk = attn[j * CHUNK:(j + 1) * CHUNK, j * CHUNK:(j + 1) * CHUNK]
                at_ref[e, pl.ds(pl.multiple_of(s * B_SUPER + j * CHUNK, CHUNK), CHUNK), :] = blk
        return carry

    lax.fori_loop(0, ns, precompute, 0)

    def step(n, states):
        rows = pl.ds(pl.multiple_of(n * CHUNK, CHUNK), CHUNK)
        new = []
        for e in range(2):
            st = states[e]
            sb = st.astype(BF16)
            v_new = u_ref[e, rows, :] - _dot(w_ref[e, rows, :].astype(BF16), sb)
            vb = v_new.astype(BF16)
            o = _dot(qg_ref[e, rows, :].astype(BF16), sb) + _dot(at_ref[e, rows, :].astype(BF16), vb)
            oo_ref[e, rows, :] = o
            el = el_ref[e, pl.ds(n * CHUNK, 1), :]
            new.append(st * el + lax.dot_general(kg_ref[e, rows, :].astype(BF16), vb, _TN,
                                                 preferred_element_type=F32))
        return tuple(new)

    zero = jnp.zeros((B_HEAD_DIM, B_HEAD_DIM), F32)
    lax.fori_loop(0, nc, step, (zero, zero))

    for e in range(2):
        z = z_ref[:, e * B_HEAD_DIM:(e + 1) * B_HEAD_DIM].astype(F32)
        y = _rms(oo_ref[e]) * gain_ref[...] * (z * jax.nn.sigmoid(z))
        o_ref[:, e * B_HEAD_DIM:(e + 1) * B_HEAD_DIM] = y.astype(o_ref.dtype)


def _gdn(qkvz, conv_w, gnat, gtr, out_gain, batch, seq):
    n = qkvz.shape[0]
    hd = B_HEAD_DIM
    kh = B_K_HEADS
    ns = seq // B_SUPER
    col = lambda f: (lambda b, h: (b, f(h)))
    wcol = lambda f: (lambda b, h: (0, f(h)))
    return pl.pallas_call(
        _gdn_kernel,
        grid=(batch, kh),
        in_specs=[pl.BlockSpec((seq, hd), col(lambda h: h)),
                  pl.BlockSpec((seq, hd), col(lambda h: kh + h)),
                  pl.BlockSpec((seq, hd), col(lambda h: 2 * kh + 2 * h)),
                  pl.BlockSpec((seq, hd), col(lambda h: 2 * kh + 2 * h + 1)),
                  pl.BlockSpec((seq, 2 * hd), col(lambda h: 2 * kh + h)),
                  pl.BlockSpec((B_CONV, hd), wcol(lambda h: h)),
                  pl.BlockSpec((B_CONV, hd), wcol(lambda h: kh + h)),
                  pl.BlockSpec((B_CONV, hd), wcol(lambda h: 2 * kh + 2 * h)),
                  pl.BlockSpec((B_CONV, hd), wcol(lambda h: 2 * kh + 2 * h + 1)),
                  pl.BlockSpec((seq, LANES), lambda b, h: (b, 0)),
                  pl.BlockSpec((1, ns, LANES, B_SUPER), lambda b, h: (b, 0, 0, 0)),
                  pl.BlockSpec((1, hd), lambda b, h: (0, 0))],
        out_specs=pl.BlockSpec((seq, 2 * hd), lambda b, h: (b, h)),
        out_shape=jax.ShapeDtypeStruct((n, B_V_HEADS * hd), BF16),
        scratch_shapes=[pltpu.VMEM((seq + 8, hd), F32),
                        pltpu.VMEM((seq, hd), F32),
                        pltpu.VMEM((seq, hd), F32),
                        pltpu.VMEM((2, seq, hd), F32),
                        pltpu.VMEM((2, seq, hd), F32),
                        pltpu.VMEM((2, seq, hd), F32),
                        pltpu.VMEM((2, seq, hd), F32),
                        pltpu.VMEM((2, seq, hd), F32),
                        pltpu.VMEM((2, seq, CHUNK), F32),
                        pltpu.VMEM((2, seq, 1), F32),
                        pltpu.VMEM((2, seq, hd), F32)],
        compiler_params=_params(("arbitrary", "arbitrary")),
        name="gated_deltanet",
    )(qkvz, qkvz, qkvz, qkvz, qkvz, conv_w, conv_w, conv_w, conv_w, gnat, gtr, out_gain)


_CAND_W = [P_TOPK // (r + 1) for r in range(P_TOPK)]
_CAND_N = sum(_CAND_W)
_CAND_PAD = -(-_CAND_N // 8) * 8


def _top16_ranks(s, iota_k=None):
    rank = jnp.full(s.shape, float(P_TOPK), F32)
    vals = []
    for r in range(P_TOPK):
        m = jnp.max(s, axis=0, keepdims=True)
        hit = s == m
        if iota_k is not None:
            hit = iota_k == jnp.min(jnp.where(hit, iota_k, P_N_KEYS), axis=0, keepdims=True)
        rank = jnp.where(hit, float(r), rank)
        s = jnp.where(hit, -jnp.inf, s)
        vals.append(m)
    return vals, rank


def _route_head(s1, s2, iota_k):
    tn = s1.shape[1]
    v1, rank1 = _top16_ranks(s1, iota_k)
    v2, rank2 = _top16_ranks(s2, iota_k)
    v2s = jnp.concatenate(v2, axis=0)
    ea = [jnp.exp(v1[r] - v1[0]) for r in range(P_TOPK)]
    eb = jnp.exp(v2s - v2[0])
    neg_pad = jnp.full((_CAND_PAD - _CAND_N, tn), -jnp.inf, F32)
    iota_c = lax.broadcasted_iota(I32, (_CAND_PAD, tn), 0)
    cand = jnp.concatenate([v1[r] + v2s[0:_CAND_W[r]] for r in range(P_TOPK)] + [neg_pad], axis=0)
    gate = jnp.concatenate([ea[r] * eb[0:_CAND_W[r]] for r in range(P_TOPK)] + [jnp.zeros_like(neg_pad)], axis=0)
    sel = jnp.zeros((_CAND_PAD, tn), F32)
    work = cand
    for _ in range(P_TOPK):
        m = jnp.max(work, axis=0, keepdims=True)
        idx = jnp.min(jnp.where(work == m, iota_c, _CAND_PAD), axis=0, keepdims=True)
        hit = iota_c == idx
        sel = jnp.where(hit, 1.0, sel)
        work = jnp.where(hit, -jnp.inf, work)
    inv_z = 1.0 / jnp.sum(sel * gate, axis=0, keepdims=True)
    cnt_t = jnp.zeros(s1.shape, F32)
    off = 0
    for r in range(P_TOPK):
        n_r = jnp.sum(sel[off:off + _CAND_W[r]], axis=0, keepdims=True)
        off += _CAND_W[r]
        cnt_t = jnp.where(rank1 == float(r), n_r, cnt_t)
    a_t = jnp.where(rank1 < float(P_TOPK), jnp.exp(s1 - v1[0]), 0.0)
    b_t = jnp.where(rank2 < float(P_TOPK), jnp.exp(s2 - v2[0]) * inv_z, 0.0)
    n_ranked = jnp.sum(jnp.where(rank1 < float(P_TOPK), 1.0, 0.0) + jnp.where(rank2 < float(P_TOPK), 1.0, 0.0),
                       axis=0, keepdims=True)
    return rank2, cnt_t, a_t, b_t, n_ranked


def _bf16_pair_words(lo, hi):
    bits = lambda v: lax.bitcast_convert_type(v.astype(BF16).astype(F32), I32)
    return bits(hi) | lax.shift_right_logical(bits(lo), 16)


def _peer_route_kernel(x_ref, sh_ref, sc_ref, wq_ref, sk_ref, ht_ref, rank_ref, cnt_ref, a_ref, b_ref):
    tn = x_ref.shape[0]
    h = _rms(x_ref[...]) * (1.0 + sc_ref[0]) + sh_ref[0]
    ht_ref[...] = h.T.astype(BF16)
    qb = _dot(h.astype(BF16), wq_ref[...]).astype(BF16)
    iota_k = lax.broadcasted_iota(I32, (P_N_KEYS, tn), 0)
    for hd in range(P_HEADS):
        base = hd * 2 * P_HALF
        s1 = _dot_nt(sk_ref[hd, 0], qb[:, base:base + P_HALF])
        s2 = _dot_nt(sk_ref[hd, 1], qb[:, base + P_HALF:base + 2 * P_HALF])

        def emit(index_ties):
            rank2, cnt_t, a_t, b_t, n_ranked = _route_head(s1, s2, iota_k if index_ties else None)
            half = P_N_KEYS // 2
            rank_ref[hd] = _bf16_pair_words(rank2[:half], rank2[half:])
            cnt_ref[hd] = _bf16_pair_words(cnt_t, cnt_t)
            a_ref[hd] = _bf16_pair_words(a_t, a_t)
            b_ref[hd] = _bf16_pair_words(b_t[:half], b_t[half:])
            return n_ranked

        n_ranked = emit(False)

        @pl.when(jnp.max(n_ranked) > float(2 * P_TOPK))
        def _():
            emit(True)


def _peer_route(x2, shift, scale, wq, sub_keys, seq, tn):
    n, d = x2.shape
    per_b = seq // tn
    tok = lambda i: (0, 0, i)
    arr = jax.ShapeDtypeStruct((P_HEADS, P_N_KEYS, n), I32)
    arr_pair = jax.ShapeDtypeStruct((P_HEADS, P_N_KEYS // 2, n), I32)
    return pl.pallas_call(
        _peer_route_kernel,
        grid=(n // tn,),
        in_specs=[pl.BlockSpec((tn, d), lambda i: (i, 0)),
                  pl.BlockSpec((1, 1, d), lambda i: (i // per_b, 0, 0)),
                  pl.BlockSpec((1, 1, d), lambda i: (i // per_b, 0, 0)),
                  pl.BlockSpec(wq.shape, lambda i: (0, 0)),
                  pl.BlockSpec(sub_keys.shape, lambda i: (0, 0, 0, 0))],
        out_specs=[pl.BlockSpec((d, tn), lambda i: (0, i)),
                   pl.BlockSpec((P_HEADS, P_N_KEYS // 2, tn), tok),
                   pl.BlockSpec((P_HEADS, P_N_KEYS, tn), tok),
                   pl.BlockSpec((P_HEADS, P_N_KEYS, tn), tok),
                   pl.BlockSpec((P_HEADS, P_N_KEYS // 2, tn), tok)],
        out_shape=[jax.ShapeDtypeStruct((d, n), BF16), arr_pair, arr, arr, arr_pair],
        compiler_params=_params(("arbitrary",)),
        name="peer_route",
    )(x2, shift, scale, wq, sub_keys)


def _gelu(x):
    return 0.5 * x * (1.0 + lax.erf(x * (2.0 ** -0.5)))


def _peer_dense_kernel(ht_ref, u_ref, vt_ref, rank_ref, cnt_ref, a_ref, b_ref, x_ref, gf_ref, fn_ref,
                       o_ref, acc_ref, act_ref, p_ref, *, final):
    e = pl.program_id(1)
    eb = u_ref.shape[0]
    n_i = eb // P_N_KEYS

    @pl.when(e == 0)
    def _():
        acc_ref[...] = jnp.zeros(acc_ref.shape, F32)

    tn = ht_ref.shape[1]
    half = P_N_KEYS // 2
    n_slab = half // SUBLANES
    zero = jnp.zeros((2 * SUBLANES, LANES), BF16)

    def words(ref, hd, rows, lanes):
        return pltpu.bitcast(ref[hd, rows, lanes], BF16)

    def row_tile(ref, hd, ii, lanes):
        return pltpu.bitcast(jnp.broadcast_to(ref[hd, 0, ii:ii + 1, lanes], (SUBLANES, LANES)), BF16)

    def gate_times_act(i_rows):
        for t in range(tn // LANES):
            lanes = slice(t * LANES, (t + 1) * LANES)
            for s0 in range(0, n_slab, P_SLAB_GROUP):
                slabs = range(s0, s0 + P_SLAB_GROUP)
                g = {(ii, s): zero for ii in i_rows for s in slabs}
                for hd in range(P_HEADS):
                    r = {s: words(rank_ref, hd, slice(s * SUBLANES, (s + 1) * SUBLANES), lanes) for s in slabs}
                    b = {s: words(b_ref, hd, slice(s * SUBLANES, (s + 1) * SUBLANES), lanes) for s in slabs}
                    for ii in i_rows:
                        cnt = row_tile(cnt_ref, hd, ii, lanes)
                        a = row_tile(a_ref, hd, ii, lanes)
                        for s in slabs:
                            g[ii, s] = g[ii, s] + a * jnp.where(r[s] < cnt, b[s], zero)
                for ii in i_rows:
                    for s in slabs:
                        gw = pltpu.bitcast(g[ii, s], I32)
                        lo = slice(ii * P_N_KEYS + s * SUBLANES, ii * P_N_KEYS + (s + 1) * SUBLANES)
                        hi = slice(ii * P_N_KEYS + half + s * SUBLANES, ii * P_N_KEYS + half + (s + 1) * SUBLANES)
                        g_lo = lax.bitcast_convert_type(lax.shift_left(gw, 16), F32)
                        g_hi = lax.bitcast_convert_type(gw & jnp.int32(-65536), F32)
                        p_ref[lo, lanes] = (g_lo * _gelu(act_ref[lo, lanes])).astype(BF16)
                        p_ref[hi, lanes] = (g_hi * _gelu(act_ref[hi, lanes])).astype(BF16)

    for i0 in range(0, n_i, P_SUB_ROWS):
        rows = slice(i0 * P_N_KEYS, (i0 + P_SUB_ROWS) * P_N_KEYS)
        act_ref[rows, :] = _dot(u_ref[rows, :], ht_ref[...])
    for i0 in range(0, n_i, P_SUB_ROWS):
        rows = slice(i0 * P_N_KEYS, (i0 + P_SUB_ROWS) * P_N_KEYS)
        gate_times_act(range(i0, i0 + P_SUB_ROWS))
        acc_ref[...] += _dot(vt_ref[:, rows], p_ref[rows, :])

    @pl.when(e == pl.num_programs(1) - 1)
    def _():
        xn = x_ref[...] + gf_ref[0] * acc_ref[...].T
        if final:
            xn = _rms(xn) * fn_ref[...]
        o_ref[...] = xn


def _peer_dense(ht, u, vt, rank, cnt, a, b, x2, gate, fgain, seq, tn, eb, final):
    n, d = x2.shape
    n_e = u.shape[0]
    per_b = seq // tn
    tok = lambda t, e: (0, 0, t)
    n_i = eb // P_N_KEYS
    rows_of = lambda t, e: (0, e, 0, t)
    by_rows = lambda v: v.reshape(P_HEADS, P_N_KEYS // n_i, n_i, n)
    return pl.pallas_call(
        functools.partial(_peer_dense_kernel, final=final),
        grid=(n // tn, n_e // eb),
        in_specs=[pl.BlockSpec((d, tn), lambda t, e: (0, t)),
                  pl.BlockSpec((eb, d), lambda t, e: (e, 0)),
                  pl.BlockSpec((d, eb), lambda t, e: (0, e)),
                  pl.BlockSpec((P_HEADS, P_N_KEYS // 2, tn), tok),
                  pl.BlockSpec((P_HEADS, 1, n_i, tn), rows_of),
                  pl.BlockSpec((P_HEADS, 1, n_i, tn), rows_of),
                  pl.BlockSpec((P_HEADS, P_N_KEYS // 2, tn), tok),
                  pl.BlockSpec((tn, d), lambda t, e: (t, 0)),
                  pl.BlockSpec((1, 1, d), lambda t, e: (t // per_b, 0, 0)),
                  pl.BlockSpec((1, d), lambda t, e: (0, 0))],
        out_specs=pl.BlockSpec((tn, d), lambda t, e: (t, 0)),
        out_shape=jax.ShapeDtypeStruct((n, d), F32),
        scratch_shapes=[pltpu.VMEM((d, tn), F32),
                        pltpu.VMEM((eb, tn), F32),
                        pltpu.VMEM((eb, tn), BF16)],
        compiler_params=_params(("arbitrary", "arbitrary")),
        name="peer_dense",
    )(ht, u, vt, rank, by_rows(cnt), by_rows(a), b, x2, gate, fgain)


def _peer(x2, shift, scale, gate, w_q, sub_keys, u_tab, v_tab, fgain, seq, final):
    tn = min(512, seq)
    ht, rank, cnt, a, b = _peer_route(x2, shift, scale, w_q.astype(BF16), sub_keys.astype(BF16), seq,
                                      min(256, seq))
    return _peer_dense(ht, u_tab.astype(BF16), v_tab.T.astype(BF16), rank, cnt, a, b, x2, gate, fgain,
                       seq, tn, P_EXPERT_BLOCK, final)


def kernel(x, c, a_w_in, a_kv_norm, a_w_uk, a_w_uv, a_w_out, b_w_in, b_conv, b_a_log, b_dt_bias,
           b_out_norm, b_w_out, p_w_q, p_sub_keys, p_u, p_v, ada_w, ada_b, final_norm):
    batch, seq, d = x.shape
    n = batch * seq
    depth = ada_w.shape[0]
    x2 = x.reshape(n, d)
    mod = _adaln(c, ada_w, ada_b).reshape(depth, batch, 6, 1, d)
    fgain = final_norm.reshape(1, d)
    tm = min(512, seq)
    for layer in range(depth):
        sh_m, sc_m, g_m, sh_f, sc_f, g_f = (mod[layer, :, k] for k in range(6))
        j = layer // 2
        if layer % 2 == 0:
            w_in = a_w_in[j]
            hr = A_HEADS * A_HEAD_DIM
            c0, c1, c2 = hr + A_KV_RANK, hr + A_KV_RANK + 512, hr + A_KV_RANK + 512 + A_IDX_DIM + A_IDX_HEADS
            w_a = jnp.concatenate([w_in[:, :hr], w_in[:, c0:c1], w_in[:, hr:c0], w_in[:, c1:c2],
                                   jnp.zeros((d, 1920 - c2), w_in.dtype)], axis=1).astype(BF16)
            proj = _norm_mod_matmul(x2, sh_m, sc_m, w_a, BF16, seq, tm, 1920)
            x2 = _dsa(proj, x2, g_m, a_kv_norm[j].reshape(1, -1),
                      a_w_uk[j].transpose(1, 2, 0).astype(BF16), a_w_uv[j].transpose(1, 0, 2).astype(BF16),
                      a_w_out[j].astype(BF16), batch, seq)
        else:
            w_in = b_w_in[j]
            nqkvz = 2 * B_K_HEADS * B_HEAD_DIM + 2 * B_V_HEADS * B_HEAD_DIM
            w_g = jnp.concatenate([w_in[:, nqkvz:], jnp.zeros((d, LANES - 2 * B_V_HEADS), w_in.dtype)], axis=1)
            qkvz = _norm_mod_matmul(x2, sh_m, sc_m, w_in[:, :nqkvz].astype(BF16), BF16, seq, tm, 1024)
            gates = _norm_mod_matmul(x2, sh_m, sc_m, w_g.astype(BF16), F32, seq, tm, LANES)
            gnat, gtr = _gdn_gates(gates, b_a_log[j], b_dt_bias[j], batch, seq)
            onorm = _gdn(qkvz, b_conv[j].reshape(B_CONV, -1), gnat, gtr, b_out_norm[j].reshape(1, -1), batch, seq)
            x2 = _matmul_residual(onorm, b_w_out[j].astype(BF16), x2, g_m, seq, tm)
        x2 = _peer(x2, sh_f, sc_f, g_f, p_w_q[layer], p_sub_keys[layer], p_u[layer], p_v[layer], fgain, seq,
                   final=(layer == depth - 1))
    return x2.reshape(batch, seq, d)
```

```python
import functools

import jax
import jax.numpy as jnp
from jax import lax
from jax.experimental import pallas as pl
from jax.experimental.pallas import tpu as pltpu

F32 = jnp.float32
BF16 = jnp.bfloat16
I32 = jnp.int32
HIGHEST = lax.Precision.HIGHEST

RMS_EPS = 1e-6
CHUNK = 64
LANES = 128
SUBLANES = 8
NEG_BIG = -1e30
SOFTMAX_MIN_SUM = 1e-30

A_HEADS = 16
A_HEAD_DIM = 64
A_KV_RANK = 256
A_IDX_HEADS = 8
A_IDX_DIM = 64
A_TOPK_MAX = 256
A_QBLOCK = 128
A_KTILE = 256

B_K_HEADS = 8
B_V_HEADS = 16
B_HEAD_DIM = 128
B_CONV = 4
B_SUPER = 256
B_GROUPS_IN_FLIGHT = 2

P_HEADS = 8
P_N_KEYS = 128
P_HALF = 128
P_TOPK = 16
P_TOKEN_TILE = 512
P_EXPERT_BLOCK = 1024
P_SUB_ROWS = 4
P_SLAB_GROUP = 4

INT_MIN = int(jnp.iinfo(jnp.int32).min)

_NT = (((1,), (1,)), ((), ()))
_TN = (((0,), (0,)), ((), ()))


def _vmem_limit(mib):
    return pltpu.CompilerParams(vmem_limit_bytes=mib * 1024 * 1024)


def _params(sem, mib=48):
    return pltpu.CompilerParams(dimension_semantics=sem, vmem_limit_bytes=mib * 1024 * 1024)


def _dot(a, b):
    return jnp.dot(a, b, preferred_element_type=F32)


def _dot_nt(a, b):
    return lax.dot_general(a, b, _NT, preferred_element_type=F32)


def _split_bf16(a):
    hi = a.astype(BF16)
    lo = (a - hi.astype(F32)).astype(BF16)
    return hi, lo


def _dot3(a, b):
    ah, al = _split_bf16(a)
    bh, bl = _split_bf16(b)
    return _dot(ah, bh) + (_dot(ah, bl) + _dot(al, bh))


def _rms(x):
    return x * lax.rsqrt(jnp.mean(x * x, axis=-1, keepdims=True) + RMS_EPS)


def _adaln_kernel(c_ref, w_ref, b_ref, o_ref):
    c = c_ref[...]
    cond = c * jax.nn.sigmoid(c)
    o_ref[0] = jnp.dot(cond, w_ref[0], preferred_element_type=F32, precision=HIGHEST) + b_ref[0]


def _adaln(c, ada_w, ada_b):
    depth, d, d6 = ada_w.shape
    b = c.shape[0]
    tn = 1536
    return pl.pallas_call(
        _adaln_kernel,
        grid=(depth, d6 // tn),
        in_specs=[pl.BlockSpec((b, d), lambda l, j: (0, 0)),
                  pl.BlockSpec((1, d, tn), lambda l, j: (l, 0, j)),
                  pl.BlockSpec((1, 1, tn), lambda l, j: (l, 0, j))],
        out_specs=pl.BlockSpec((1, b, tn), lambda l, j: (l, 0, j)),
        out_shape=jax.ShapeDtypeStruct((depth, b, d6), F32),
        compiler_params=_params(("arbitrary", "arbitrary")),
        name="adaln",
    )(c, ada_w, ada_b.reshape(depth, 1, d6))


def _nmm_kernel(x_ref, sh_ref, sc_ref, w_ref, o_ref, h_ref):
    @pl.when(pl.program_id(1) == 0)
    def _():
        h = _rms(x_ref[...]) * (1.0 + sc_ref[0]) + sh_ref[0]
        h_ref[...] = h.astype(BF16)

    o_ref[...] = _dot(h_ref[...], w_ref[...]).astype(o_ref.dtype)


def _norm_mod_matmul(x2, shift, scale, w, out_dtype, seq, tm, tn):
    n, d = x2.shape
    n_out = w.shape[1]
    per_b = seq // tm
    return pl.pallas_call(
        _nmm_kernel,
        grid=(n // tm, n_out // tn),
        in_specs=[pl.BlockSpec((tm, d), lambda i, j: (i, 0)),
                  pl.BlockSpec((1, 1, d), lambda i, j: (i // per_b, 0, 0)),
                  pl.BlockSpec((1, 1, d), lambda i, j: (i // per_b, 0, 0)),
                  pl.BlockSpec((d, tn), lambda i, j: (0, j))],
        out_specs=pl.BlockSpec((tm, tn), lambda i, j: (i, j)),
        out_shape=jax.ShapeDtypeStruct((n, n_out), out_dtype),
        scratch_shapes=[pltpu.VMEM((tm, d), BF16)],
        compiler_params=_params(("arbitrary", "arbitrary")),
        name="norm_mod_matmul",
    )(x2, shift, scale, w)


def _mmres_kernel(a_ref, w_ref, x_ref, g_ref, o_ref):
    o_ref[...] = x_ref[...] + g_ref[0] * _dot(a_ref[...], w_ref[...])


def _matmul_residual(a, w, x2, gate, seq, tm):
    n, k = a.shape
    d = w.shape[1]
    per_b = seq // tm
    return pl.pallas_call(
        _mmres_kernel,
        grid=(n // tm,),
        in_specs=[pl.BlockSpec((tm, k), lambda i: (i, 0)),
                  pl.BlockSpec((k, d), lambda i: (0, 0)),
                  pl.BlockSpec((tm, d), lambda i: (i, 0)),
                  pl.BlockSpec((1, 1, d), lambda i: (i // per_b, 0, 0))],
        out_specs=pl.BlockSpec((tm, d), lambda i: (i, 0)),
        out_shape=jax.ShapeDtypeStruct((n, d), F32),
        compiler_params=_params(("arbitrary",)),
        name="matmul_residual",
    )(a, w, x2, gate)


def _dsa_kernel(q_ref, qidx_ref, kwq_ref, ckv_ref, kw_ref, gain_ref, wuk_ref, wuv_ref, wout_ref,
                x_ref, gm_ref, o_ref,
                ckvn_ref, kn_ref, keys_ref, ql_ref, m_ref, l_ref, acc_ref, *, n_sel):
    qi = pl.program_id(1)
    tq = A_QBLOCK
    n_kt = qi + 1
    seq = keys_ref.shape[1]

    @pl.when(qi == 0)
    def _():
        cn = (_rms(ckv_ref[...].astype(F32)) * gain_ref[...]).astype(BF16)
        ckvn_ref[...] = cn
        cf = cn.astype(F32)
        kn2 = jnp.max(jnp.sum(cf * cf, axis=1, keepdims=True), axis=0, keepdims=True)
        kn_ref[...] = jnp.broadcast_to(kn2, kn_ref.shape)

    row = lax.broadcasted_iota(I32, (tq, LANES), 0)
    lane = lax.broadcasted_iota(I32, (tq, LANES), 1)
    qh = [qidx_ref[:, h * A_IDX_DIM:(h + 1) * A_IDX_DIM] for h in range(A_IDX_HEADS)]
    wq = kwq_ref[:, A_IDX_DIM:A_IDX_DIM + A_IDX_HEADS].astype(F32) * ((A_IDX_HEADS ** -0.5) * (A_IDX_DIM ** -0.5))
    wb = [jnp.broadcast_to(wq[:, h:h + 1], (tq, LANES)) for h in range(A_IDX_HEADS)]
    diag_inadm = (row < CHUNK) & (lane >= CHUNK)

    def score_tile(kt, carry):
        off = pl.multiple_of(kt * LANES, LANES)
        kt_k = kw_ref[pl.ds(off, LANES), 0:A_IDX_DIM]
        sc = jnp.zeros((tq, LANES), F32)
        for h in range(A_IDX_HEADS):
            sc = sc + wb[h] * jnp.maximum(_dot_nt(qh[h], kt_k), 0.0)
        bits = lax.bitcast_convert_type(sc, I32)
        key = jnp.where(bits < 0, bits ^ jnp.int32(0x7FFFFFFF), bits)
        key = jnp.where(diag_inadm & (kt == qi), jnp.int32(INT_MIN), key)
        keys_ref[:, pl.ds(off, LANES)] = key
        return carry

    lax.fori_loop(0, n_kt, score_tile, 0)

    @pl.when(n_kt * LANES < seq)
    def _():
        keys_ref[:, pl.ds(pl.multiple_of(n_kt * LANES, LANES), LANES)] = jnp.full((tq, LANES), INT_MIN, I32)

    n_kt2 = (n_kt + 1) // 2
    lane_k = lax.broadcasted_iota(I32, (tq, A_KTILE), 1)

    def count(pred):
        def body(t, acc):
            off = pl.multiple_of(t * A_KTILE, A_KTILE)
            return acc + jnp.where(pred(keys_ref[:, pl.ds(off, A_KTILE)], off), 1.0, 0.0)
        acc = lax.fori_loop(0, n_kt2, body, jnp.zeros((tq, A_KTILE), F32))
        return jnp.sum(acc, axis=1, keepdims=True)

    def count_ge(cand):
        cb = jnp.broadcast_to(cand, (tq, A_KTILE))
        return count(lambda k, off: k >= cb)

    kf = float(n_sel)
    zero = jnp.zeros((tq, 1), I32)
    ans = jnp.where(count_ge(zero) >= kf, zero, jnp.int32(INT_MIN))

    def bit_body(b, ans):
        cand = ans + jnp.left_shift(jnp.int32(1), 30 - b)
        return jnp.where(count_ge(cand) >= kf, cand, ans)

    ans = lax.fori_loop(0, 31, bit_body, ans)
    thr = jnp.maximum(ans, jnp.int32(INT_MIN + 1))
    thr_k = jnp.broadcast_to(thr, (tq, A_KTILE))
    n_gt = count(lambda k, off: k > thr_k)
    n_ge = count(lambda k, off: k >= thr_k)
    need = kf - n_gt

    m_ref[0:tq, :] = jnp.full((tq, 1), float(2 * seq), F32)

    @pl.when(jnp.max(n_ge) > kf)
    def _():
        def eq_before(bound):
            bb = jnp.broadcast_to(bound, (tq, A_KTILE))
            return count(lambda k, off: (k == thr_k) & ((off + lane_k).astype(F32) < bb))

        nbits = max(1, (seq - 1).bit_length())

        def jb(b, bound):
            cand = bound + jnp.left_shift(jnp.int32(1), nbits - b).astype(F32)
            return jnp.where(eq_before(cand) <= need, cand, bound)

        m_ref[0:tq, :] = lax.fori_loop(0, nbits + 1, jb, jnp.zeros((tq, 1), F32))

    bound_b = jnp.broadcast_to(m_ref[0:tq, :], (tq, A_KTILE))

    qn2 = jnp.zeros((tq, 1), F32)
    for h in range(A_HEADS):
        qlat = _dot(q_ref[:, h * A_HEAD_DIM:(h + 1) * A_HEAD_DIM], wuk_ref[h]) * (A_HEAD_DIM ** -0.5)
        qlb = qlat.astype(BF16)
        ql_ref[h * tq:(h + 1) * tq, :] = qlb
        qf = qlb.astype(F32)
        qn2 = jnp.maximum(qn2, jnp.sum(qf * qf, axis=1, keepdims=True))

    def selected(key, off):
        return (key > thr_k) | ((key == thr_k) & ((off + lane_k).astype(F32) < bound_b))

    logit_cap = jnp.sqrt(qn2 * kn_ref[:, 0:1]) * (1.0 + 2.0 ** -7) + 2.0 ** -20
    cap_b = jnp.broadcast_to(logit_cap, (tq, A_KTILE))
    l_ref[...] = jnp.zeros(l_ref.shape, F32)
    acc_ref[...] = jnp.zeros(acc_ref.shape, F32)

    def att_tile_capped(t, carry):
        off = pl.multiple_of(t * A_KTILE, A_KTILE)
        kv = ckvn_ref[pl.ds(off, A_KTILE), :]
        bias = jnp.where(selected(keys_ref[:, pl.ds(off, A_KTILE)], off), -cap_b, NEG_BIG)
        p = jnp.exp(_dot_nt(ql_ref[...], kv).reshape(A_HEADS, tq, A_KTILE) + bias[None])
        l_ref[...] += jnp.sum(p, axis=2, keepdims=True).reshape(A_HEADS * tq, 1)
        acc_ref[...] += _dot(p.reshape(A_HEADS * tq, A_KTILE).astype(BF16), kv)
        return carry

    lax.fori_loop(0, n_kt2, att_tile_capped, 0)

    @pl.when(jnp.logical_not(jnp.min(l_ref[...]) > SOFTMAX_MIN_SUM))
    def _():
        m_ref[...] = jnp.full(m_ref.shape, NEG_BIG, F32)
        l_ref[...] = jnp.zeros(l_ref.shape, F32)
        acc_ref[...] = jnp.zeros(acc_ref.shape, F32)

        def att_tile(t, carry):
            off = pl.multiple_of(t * A_KTILE, A_KTILE)
            kv = ckvn_ref[pl.ds(off, A_KTILE), :]
            bias = jnp.where(selected(keys_ref[:, pl.ds(off, A_KTILE)], off), 0.0, NEG_BIG)
            s = _dot_nt(ql_ref[...], kv).reshape(A_HEADS, tq, A_KTILE) + bias[None]
            m_old = m_ref[...].reshape(A_HEADS, tq, 1)
            m_new = jnp.maximum(m_old, jnp.max(s, axis=2, keepdims=True))
            p = jnp.exp(s - m_new)
            alpha = jnp.exp(m_old - m_new)
            l_ref[...] = (alpha * l_ref[...].reshape(A_HEADS, tq, 1)
                          + jnp.sum(p, axis=2, keepdims=True)).reshape(A_HEADS * tq, 1)
            m_ref[...] = m_new.reshape(A_HEADS * tq, 1)
            pv = _dot(p.reshape(A_HEADS * tq, A_KTILE).astype(BF16), kv)
            acc_ref[...] = alpha.reshape(A_HEADS * tq, 1) * acc_ref[...] + pv
            return carry

        lax.fori_loop(0, n_kt2, att_tile, 0)

    outs = []
    for h in range(A_HEADS):
        rs = slice(h * tq, (h + 1) * tq)
        o_lat = acc_ref[rs, :] / l_ref[rs, :]
        outs.append(_dot(o_lat.astype(BF16), wuv_ref[h]))
    att = jnp.concatenate(outs, axis=1)
    o_ref[...] = x_ref[...] + gm_ref[0] * _dot(att.astype(BF16), wout_ref[...])


def _dsa(proj, x2, gate, kv_gain, wuk_t, wuv, wout, batch, seq):
    n, d = x2.shape
    nq = seq // A_QBLOCK
    hr = A_HEADS * A_HEAD_DIM
    n_sel = min(A_TOPK_MAX, seq // 4)
    return pl.pallas_call(
        functools.partial(_dsa_kernel, n_sel=n_sel),
        grid=(batch, nq),
        in_specs=[pl.BlockSpec((A_QBLOCK, hr), lambda b, i: (b * nq + i, 0)),
                  pl.BlockSpec((A_QBLOCK, 512), lambda b, i: (b * nq + i, 2)),
                  pl.BlockSpec((A_QBLOCK, LANES), lambda b, i: (b * nq + i, 14)),
                  pl.BlockSpec((seq, A_KV_RANK), lambda b, i: (b, 6)),
                  pl.BlockSpec((seq, LANES), lambda b, i: (b, 14)),
                  pl.BlockSpec((1, A_KV_RANK), lambda b, i: (0, 0)),
                  pl.BlockSpec((A_HEADS, A_HEAD_DIM, A_KV_RANK), lambda b, i: (0, 0, 0)),
                  pl.BlockSpec((A_HEADS, A_KV_RANK, A_HEAD_DIM), lambda b, i: (0, 0, 0)),
                  pl.BlockSpec((hr, d), lambda b, i: (0, 0)),
                  pl.BlockSpec((A_QBLOCK, d), lambda b, i: (b * nq + i, 0)),
                  pl.BlockSpec((1, 1, d), lambda b, i: (b, 0, 0))],
        out_specs=pl.BlockSpec((A_QBLOCK, d), lambda b, i: (b * nq + i, 0)),
        out_shape=jax.ShapeDtypeStruct((n, d), F32),
        scratch_shapes=[pltpu.VMEM((seq, A_KV_RANK), BF16),
                        pltpu.VMEM((1, LANES), F32),
                        pltpu.VMEM((A_QBLOCK, seq), I32),
                        pltpu.VMEM((A_HEADS * A_QBLOCK, A_KV_RANK), BF16),
                        pltpu.VMEM((A_HEADS * A_QBLOCK, 1), F32),
                        pltpu.VMEM((A_HEADS * A_QBLOCK, 1), F32),
                        pltpu.VMEM((A_HEADS * A_QBLOCK, A_KV_RANK), F32)],
        compiler_params=_params(("arbitrary", "arbitrary")),
        name="dsa_attention",
    )(proj, proj, proj, proj, proj, kv_gain, wuk_t, wuv, wout, x2, gate)


def _gdn_gate_kernel(g_ref, alog_ref, dtb_ref, nat_ref, tr_ref):
    g = g_ref[...]
    col = lax.broadcasted_iota(I32, g.shape, 1)
    beta = jax.nn.sigmoid(g)
    z = g + dtb_ref[...]
    softplus = jnp.maximum(z, 0.0) + jnp.log1p(jnp.exp(-jnp.abs(z)))
    gd = -jnp.exp(alog_ref[...]) * softplus
    r = lax.broadcasted_iota(I32, (B_SUPER, B_SUPER), 0)
    c = lax.broadcasted_iota(I32, (B_SUPER, B_SUPER), 1)
    same = (r >> 6) == (c >> 6)
    tril = jnp.where(same & (c <= r), 1.0, 0.0).astype(F32)
    gc = jnp.dot(tril, gd, preferred_element_type=F32, precision=HIGHEST)
    nat = jnp.where(col < B_V_HEADS, beta, gc)
    nat_ref[...] = nat
    tr_ref[0, 0] = nat.T


def _gdn_gates(gates, a_log, dt_bias, batch, seq):
    n = gates.shape[0]
    ns = seq // B_SUPER
    pad = lambda v: jnp.zeros((1, LANES), F32).at[0, B_V_HEADS:2 * B_V_HEADS].set(v.astype(F32))
    return pl.pallas_call(
        _gdn_gate_kernel,
        grid=(batch, ns),
        in_specs=[pl.BlockSpec((B_SUPER, LANES), lambda b, s: (b * ns + s, 0)),
                  pl.BlockSpec((1, LANES), lambda b, s: (0, 0)),
                  pl.BlockSpec((1, LANES), lambda b, s: (0, 0))],
        out_specs=[pl.BlockSpec((B_SUPER, LANES), lambda b, s: (b * ns + s, 0)),
                   pl.BlockSpec((1, 1, LANES, B_SUPER), lambda b, s: (b, s, 0, 0))],
        out_shape=[jax.ShapeDtypeStruct((n, LANES), F32),
                   jax.ShapeDtypeStruct((batch, ns, LANES, B_SUPER), F32)],
        compiler_params=_params(("arbitrary", "arbitrary")),
        name="gdn_gates",
    )(gates, pad(a_log), pad(dt_bias))


def _conv_silu(raw_ref, w_ref, pad_ref):
    t = raw_ref.shape[0]
    pad_ref[0:8, :] = jnp.zeros((8, LANES), F32)
    pad_ref[8:8 + t, :] = raw_ref[...].astype(F32)
    w = w_ref[...]
    y = w[3:4] * pad_ref[8:8 + t, :]
    for j in range(B_CONV - 1):
        y = y + w[j:j + 1] * pad_ref[5 + j:5 + j + t, :]
    return y * jax.nn.sigmoid(y)


def _l2n(y):
    return y * lax.rsqrt(jnp.sum(y * y, axis=-1, keepdims=True) + RMS_EPS)


def _gdn_kernel(q_ref, k_ref, v0_ref, v1_ref, z_ref, wq_ref, wk_ref, wv0_ref, wv1_ref,
                gnat_ref, gtr_ref, gain_ref, o_ref,
                pad_ref, qn_ref, kn_ref, vs_ref, u_ref, w_ref, qg_ref, kg_ref, at_ref, el_ref, oo_ref):
    hk = pl.program_id(1)
    seq = q_ref.shape[0]
    ns = seq // B_SUPER
    nc = seq // CHUNK

    qn_ref[...] = _l2n(_conv_silu(q_ref, wq_ref, pad_ref)) * (B_HEAD_DIM ** -0.5)
    kn_ref[...] = _l2n(_conv_silu(k_ref, wk_ref, pad_ref))
    vs_ref[0] = _conv_silu(v0_ref, wv0_ref, pad_ref)
    vs_ref[1] = _conv_silu(v1_ref, wv1_ref, pad_ref)

    r = lax.broadcasted_iota(I32, (B_SUPER, B_SUPER), 0)
    c = lax.broadcasted_iota(I32, (B_SUPER, B_SUPER), 1)
    same = (r >> 6) == (c >> 6)
    tril = same & (c <= r)
    strict = same & (c < r)
    is_last = c == (r | (CHUNK - 1))
    eye = jnp.where(r == c, 1.0, 0.0).astype(F32)
    lane = lax.broadcasted_iota(I32, (B_SUPER, LANES), 1)

    def precompute(s2, carry):
        chains = []
        for j in range(B_GROUPS_IN_FLIGHT):
            s = s2 * B_GROUPS_IN_FLIGHT + j
            rows = pl.ds(pl.multiple_of(s * B_SUPER, B_SUPER), B_SUPER)
            kn = kn_ref[rows, :]
            qn = qn_ref[rows, :]
            knb = kn.astype(BF16)
            kk = _dot_nt(knb, knb)
            qk = _dot_nt(qn.astype(BF16), knb)
            gnat = gnat_ref[rows, :]
            for e in range(2):
                hv = 2 * hk + e
                bcol = jnp.sum(jnp.where(lane == hv, gnat, 0.0), axis=1, keepdims=True)
                gcol = jnp.sum(jnp.where(lane == hv + B_V_HEADS, gnat, 0.0), axis=1, keepdims=True)
                grow = gtr_ref[0, s, pl.ds(hv + B_V_HEADS, 1), :]
                grow_b = jnp.broadcast_to(grow, (B_SUPER, B_SUPER))
                dec = jnp.where(tril, jnp.exp(gcol - grow_b), 0.0)
                a = jnp.where(strict, bcol * kk * dec, 0.0)
                eg = jnp.exp(gcol)
                glast = jnp.sum(jnp.where(is_last, grow_b, 0.0), axis=1, keepdims=True)
                qg_ref[e, rows, :] = qn * eg
                kg_ref[e, rows, :] = kn * jnp.exp(glast - gcol)
                el_ref[e, rows, :] = jnp.exp(glast)
                attn = jnp.where(tril, qk * dec, 0.0)
                for c in range(B_SUPER // CHUNK):
                    blk = attn[c * CHUNK:(c + 1) * CHUNK, c * CHUNK:(c + 1) * CHUNK]
                    at_ref[e, pl.ds(pl.multiple_of(s * B_SUPER + c * CHUNK, CHUNK), CHUNK), :] = blk
                chains.append(dict(e=e, rows=rows, a=a, vb=vs_ref[e, rows, :] * bcol, kb=kn * (bcol * eg)))
        for ch in chains:
            ch["pw"] = -ch["a"]
            ch["tm"] = eye + ch["pw"]
        for _ in range(5):
            for ch in chains:
                pwb = ch["pw"].astype(BF16)
                ch["pw"] = _dot(pwb, pwb)
            for ch in chains:
                ch["tm"] = ch["tm"] + _dot(ch["tm"].astype(BF16), ch["pw"].astype(BF16))
        for ch in chains:
            ch["resid"] = eye - _dot3(eye + ch["a"], ch["tm"])
        for ch in chains:
            ch["tm"] = ch["tm"] + _dot(ch["tm"].astype(BF16), ch["resid"].astype(BF16))
        for ch in chains:
            u_ref[ch["e"], ch["rows"], :] = _dot3(ch["tm"], ch["vb"])
            w_ref[ch["e"], ch["rows"], :] = _dot3(ch["tm"], ch["kb"])
        return carry

    lax.fori_loop(0, ns // B_GROUPS_IN_FLIGHT, precompute, 0)

    def step(n, states):
        rows = pl.ds(pl.multiple_of(n * CHUNK, CHUNK), CHUNK)
        new = []
        for e in range(2):
            st = states[e]
            sb = st.astype(BF16)
            v_new = u_ref[e, rows, :] - _dot(w_ref[e, rows, :].astype(BF16), sb)
            vb = v_new.astype(BF16)
            o = _dot(qg_ref[e, rows, :].astype(BF16), sb) + _dot(at_ref[e, rows, :].astype(BF16), vb)
            oo_ref[e, rows, :] = o
            el = el_ref[e, pl.ds(n * CHUNK, 1), :]
            new.append(st * el + lax.dot_general(kg_ref[e, rows, :].astype(BF16), vb, _TN,
                                                 preferred_element_type=F32))
        return tuple(new)

    zero = jnp.zeros((B_HEAD_DIM, B_HEAD_DIM), F32)
    lax.fori_loop(0, nc, step, (zero, zero))

    for e in range(2):
        z = z_ref[:, e * B_HEAD_DIM:(e + 1) * B_HEAD_DIM].astype(F32)
        y = _rms(oo_ref[e]) * gain_ref[...] * (z * jax.nn.sigmoid(z))
        o_ref[:, e * B_HEAD_DIM:(e + 1) * B_HEAD_DIM] = y.astype(o_ref.dtype)


def _gdn(qkvz, conv_w, gnat, gtr, out_gain, batch, seq):
    n = qkvz.shape[0]
    hd = B_HEAD_DIM
    kh = B_K_HEADS
    ns = seq // B_SUPER
    col = lambda f: (lambda b, h: (b, f(h)))
    wcol = lambda f: (lambda b, h: (0, f(h)))
    return pl.pallas_call(
        _gdn_kernel,
        grid=(batch, kh),
        in_specs=[pl.BlockSpec((seq, hd), col(lambda h: h)),
                  pl.BlockSpec((seq, hd), col(lambda h: kh + h)),
                  pl.BlockSpec((seq, hd), col(lambda h: 2 * kh + 2 * h)),
                  pl.BlockSpec((seq, hd), col(lambda h: 2 * kh + 2 * h + 1)),
                  pl.BlockSpec((seq, 2 * hd), col(lambda h: 2 * kh + h)),
                  pl.BlockSpec((B_CONV, hd), wcol(lambda h: h)),
                  pl.BlockSpec((B_CONV, hd), wcol(lambda h: kh + h)),
                  pl.BlockSpec((B_CONV, hd), wcol(lambda h: 2 * kh + 2 * h)),
                  pl.BlockSpec((B_CONV, hd), wcol(lambda h: 2 * kh + 2 * h + 1)),
                  pl.BlockSpec((seq, LANES), lambda b, h: (b, 0)),
                  pl.BlockSpec((1, ns, LANES, B_SUPER), lambda b, h: (b, 0, 0, 0)),
                  pl.BlockSpec((1, hd), lambda b, h: (0, 0))],
        out_specs=pl.BlockSpec((seq, 2 * hd), lambda b, h: (b, h)),
        out_shape=jax.ShapeDtypeStruct((n, B_V_HEADS * hd), BF16),
        scratch_shapes=[pltpu.VMEM((seq + 8, hd), F32),
                        pltpu.VMEM((seq, hd), F32),
                        pltpu.VMEM((seq, hd), F32),
                        pltpu.VMEM((2, seq, hd), F32),
                        pltpu.VMEM((2, seq, hd), F32),
                        pltpu.VMEM((2, seq, hd), F32),
                        pltpu.VMEM((2, seq, hd), F32),
                        pltpu.VMEM((2, seq, hd), F32),
                        pltpu.VMEM((2, seq, CHUNK), F32),
                        pltpu.VMEM((2, seq, 1), F32),
                        pltpu.VMEM((2, seq, hd), F32)],
        compiler_params=_params(("arbitrary", "arbitrary")),
        name="gated_deltanet",
    )(qkvz, qkvz, qkvz, qkvz, qkvz, conv_w, conv_w, conv_w, conv_w, gnat, gtr, out_gain)


_CAND_W = [P_TOPK // (r + 1) for r in range(P_TOPK)]
_CAND_N = sum(_CAND_W)
_CAND_PAD = -(-_CAND_N // 8) * 8


def _top16_ranks(s, iota_k=None):
    rank = jnp.full(s.shape, float(P_TOPK), F32)
    vals = []
    for r in range(P_TOPK):
        m = jnp.max(s, axis=0, keepdims=True)
        hit = s == m
        if iota_k is not None:
            hit = iota_k == jnp.min(jnp.where(hit, iota_k, P_N_KEYS), axis=0, keepdims=True)
        rank = jnp.where(hit, float(r), rank)
        s = jnp.where(hit, -jnp.inf, s)
        vals.append(m)
    return vals, rank


def _route_head(s1, s2, iota_k):
    tn = s1.shape[1]
    v1, rank1 = _top16_ranks(s1, iota_k)
    v2, rank2 = _top16_ranks(s2, iota_k)
    v2s = jnp.concatenate(v2, axis=0)
    ea = [jnp.exp(v1[r] - v1[0]) for r in range(P_TOPK)]
    eb = jnp.exp(v2s - v2[0])
    neg_pad = jnp.full((_CAND_PAD - _CAND_N, tn), -jnp.inf, F32)
    iota_c = lax.broadcasted_iota(I32, (_CAND_PAD, tn), 0)
    cand = jnp.concatenate([v1[r] + v2s[0:_CAND_W[r]] for r in range(P_TOPK)] + [neg_pad], axis=0)
    gate = jnp.concatenate([ea[r] * eb[0:_CAND_W[r]] for r in range(P_TOPK)] + [jnp.zeros_like(neg_pad)], axis=0)
    sel = jnp.zeros((_CAND_PAD, tn), F32)
    work = cand
    for _ in range(P_TOPK):
        m = jnp.max(work, axis=0, keepdims=True)
        idx = jnp.min(jnp.where(work == m, iota_c, _CAND_PAD), axis=0, keepdims=True)
        hit = iota_c == idx
        sel = jnp.where(hit, 1.0, sel)
        work = jnp.where(hit, -jnp.inf, work)
    inv_z = 1.0 / jnp.sum(sel * gate, axis=0, keepdims=True)
    cnt_t = jnp.zeros(s1.shape, F32)
    off = 0
    for r in range(P_TOPK):
        n_r = jnp.sum(sel[off:off + _CAND_W[r]], axis=0, keepdims=True)
        off += _CAND_W[r]
        cnt_t = jnp.where(rank1 == float(r), n_r, cnt_t)
    a_t = jnp.where(rank1 < float(P_TOPK), jnp.exp(s1 - v1[0]), 0.0)
    b_t = jnp.where(rank2 < float(P_TOPK), jnp.exp(s2 - v2[0]) * inv_z, 0.0)
    n_ranked = jnp.sum(jnp.where(rank1 < float(P_TOPK), 1.0, 0.0) + jnp.where(rank2 < float(P_TOPK), 1.0, 0.0),
                       axis=0, keepdims=True)
    return rank2, cnt_t, a_t, b_t, n_ranked


def _bf16_pair_words(lo, hi):
    bits = lambda v: lax.bitcast_convert_type(v.astype(BF16).astype(F32), I32)
    return bits(hi) | lax.shift_right_logical(bits(lo), 16)


def _peer_route_kernel(x_ref, sh_ref, sc_ref, wq_ref, sk_ref, ht_ref, rank_ref, cnt_ref, a_ref, b_ref):
    tn = x_ref.shape[0]
    h = _rms(x_ref[...]) * (1.0 + sc_ref[0]) + sh_ref[0]
    ht_ref[...] = h.T.astype(BF16)
    qb = _dot(h.astype(BF16), wq_ref[...]).astype(BF16)
    iota_k = lax.broadcasted_iota(I32, (P_N_KEYS, LANES), 0)
    half = P_N_KEYS // 2
    for hd in range(P_HEADS):
        base = hd * 2 * P_HALF
        s1 = _dot_nt(sk_ref[hd, 0], qb[:, base:base + P_HALF])
        s2 = _dot_nt(sk_ref[hd, 1], qb[:, base + P_HALF:base + 2 * P_HALF])

        def emit(index_ties):
            most_ranked = jnp.zeros((), F32)
            for t in range(tn // LANES):
                lanes = slice(t * LANES, (t + 1) * LANES)
                rank2, cnt_t, a_t, b_t, n_ranked = _route_head(s1[:, lanes], s2[:, lanes],
                                                               iota_k if index_ties else None)
                rank_ref[hd, :, lanes] = _bf16_pair_words(rank2[:half], rank2[half:])
                cnt_ref[hd, :, lanes] = _bf16_pair_words(cnt_t, cnt_t)
                a_ref[hd, :, lanes] = _bf16_pair_words(a_t, a_t)
                b_ref[hd, :, lanes] = _bf16_pair_words(b_t[:half], b_t[half:])
                most_ranked = jnp.maximum(most_ranked, jnp.max(n_ranked))
            return most_ranked

        most_ranked = emit(False)

        @pl.when(most_ranked > float(2 * P_TOPK))
        def _():
            emit(True)


def _peer_route(x2, shift, scale, wq, sub_keys, seq, tn):
    n, d = x2.shape
    per_b = seq // tn
    tok = lambda i: (0, 0, i)
    arr = jax.ShapeDtypeStruct((P_HEADS, P_N_KEYS, n), I32)
    arr_pair = jax.ShapeDtypeStruct((P_HEADS, P_N_KEYS // 2, n), I32)
    return pl.pallas_call(
        _peer_route_kernel,
        grid=(n // tn,),
        in_specs=[pl.BlockSpec((tn, d), lambda i: (i, 0)),
                  pl.BlockSpec((1, 1, d), lambda i: (i // per_b, 0, 0)),
                  pl.BlockSpec((1, 1, d), lambda i: (i // per_b, 0, 0)),
                  pl.BlockSpec(wq.shape, lambda i: (0, 0)),
                  pl.BlockSpec(sub_keys.shape, lambda i: (0, 0, 0, 0))],
        out_specs=[pl.BlockSpec((d, tn), lambda i: (0, i)),
                   pl.BlockSpec((P_HEADS, P_N_KEYS // 2, tn), tok),
                   pl.BlockSpec((P_HEADS, P_N_KEYS, tn), tok),
                   pl.BlockSpec((P_HEADS, P_N_KEYS, tn), tok),
                   pl.BlockSpec((P_HEADS, P_N_KEYS // 2, tn), tok)],
        out_shape=[jax.ShapeDtypeStruct((d, n), BF16), arr_pair, arr, arr, arr_pair],
        compiler_params=_params(("arbitrary",)),
        name="peer_route",
    )(x2, shift, scale, wq, sub_keys)


def _gelu(x):
    return 0.5 * x * (1.0 + lax.erf(x * (2.0 ** -0.5)))


def _peer_dense_kernel(ht_ref, u_ref, vt_ref, rank_ref, cnt_ref, a_ref, b_ref, x_ref, gf_ref, fn_ref,
                       o_ref, acc_ref, act_ref, p_ref, g_ref, *, final):
    e = pl.program_id(1)
    eb = u_ref.shape[0]
    n_i = eb // P_N_KEYS

    @pl.when(e == 0)
    def _():
        acc_ref[...] = jnp.zeros(acc_ref.shape, F32)

    tn = ht_ref.shape[1]
    half = P_N_KEYS // 2
    n_slab = half // SUBLANES
    zero = jnp.zeros((2 * SUBLANES, LANES), BF16)

    def words(ref, hd, rows, lanes):
        return pltpu.bitcast(ref[hd, rows, lanes], BF16)

    def row_tile(ref, hd, ii, lanes):
        return pltpu.bitcast(jnp.broadcast_to(ref[hd, 0, ii:ii + 1, lanes], (SUBLANES, LANES)), BF16)

    def gate_build(i_rows):
        for t in range(tn // LANES):
            lanes = slice(t * LANES, (t + 1) * LANES)
            for s0 in range(0, n_slab, P_SLAB_GROUP):
                slabs = range(s0, s0 + P_SLAB_GROUP)
                g = {(ii, s): zero for ii in i_rows for s in slabs}
                for hd in range(P_HEADS):
                    r = {s: words(rank_ref, hd, slice(s * SUBLANES, (s + 1) * SUBLANES), lanes) for s in slabs}
                    b = {s: words(b_ref, hd, slice(s * SUBLANES, (s + 1) * SUBLANES), lanes) for s in slabs}
                    for ii in i_rows:
                        cnt = row_tile(cnt_ref, hd, ii, lanes)
                        a = row_tile(a_ref, hd, ii, lanes)
                        for s in slabs:
                            g[ii, s] = g[ii, s] + a * jnp.where(r[s] < cnt, b[s], zero)
                for ii in i_rows:
                    for s in slabs:
                        wrows = slice(ii * half + s * SUBLANES, ii * half + (s + 1) * SUBLANES)
                        g_ref[wrows, lanes] = pltpu.bitcast(g[ii, s], I32)

    def apply_act(i_rows):
        for ii in i_rows:
            gw = g_ref[ii * half:(ii + 1) * half, :]
            lo = slice(ii * P_N_KEYS, ii * P_N_KEYS + half)
            hi = slice(ii * P_N_KEYS + half, (ii + 1) * P_N_KEYS)
            g_lo = lax.bitcast_convert_type(lax.shift_left(gw, 16), F32)
            g_hi = lax.bitcast_convert_type(gw & jnp.int32(-65536), F32)
            p_ref[lo, :] = (g_lo * _gelu(act_ref[lo, :])).astype(BF16)
            p_ref[hi, :] = (g_hi * _gelu(act_ref[hi, :])).astype(BF16)

    subs = [range(i0, i0 + P_SUB_ROWS) for i0 in range(0, n_i, P_SUB_ROWS)]
    rows_of = lambda sub: slice(sub[0] * P_N_KEYS, (sub[-1] + 1) * P_N_KEYS)
    gate_build(subs[0])
    for sub in subs:
        act_ref[rows_of(sub), :] = _dot(u_ref[rows_of(sub), :], ht_ref[...])
    for k, sub in enumerate(subs):
        apply_act(sub)
        if k + 1 < len(subs):
            gate_build(subs[k + 1])
        acc_ref[...] += _dot(vt_ref[:, rows_of(sub)], p_ref[rows_of(sub), :])

    @pl.when(e == pl.num_programs(1) - 1)
    def _():
        xn = x_ref[...] + gf_ref[0] * acc_ref[...].T
        if final:
            xn = _rms(xn) * fn_ref[...]
        o_ref[...] = xn


def _peer_dense(ht, u, vt, rank, cnt, a, b, x2, gate, fgain, seq, tn, eb, final):
    n, d = x2.shape
    n_e = u.shape[0]
    per_b = seq // tn
    tok = lambda t, e: (0, 0, t)
    n_i = eb // P_N_KEYS
    rows_of = lambda t, e: (0, e, 0, t)
    by_rows = lambda v: v.reshape(P_HEADS, P_N_KEYS // n_i, n_i, n)
    return pl.pallas_call(
        functools.partial(_peer_dense_kernel, final=final),
        grid=(n // tn, n_e // eb),
        in_specs=[pl.BlockSpec((d, tn), lambda t, e: (0, t)),
                  pl.BlockSpec((eb, d), lambda t, e: (e, 0)),
                  pl.BlockSpec((d, eb), lambda t, e: (0, e)),
                  pl.BlockSpec((P_HEADS, P_N_KEYS // 2, tn), tok),
                  pl.BlockSpec((P_HEADS, 1, n_i, tn), rows_of),
                  pl.BlockSpec((P_HEADS, 1, n_i, tn), rows_of),
                  pl.BlockSpec((P_HEADS, P_N_KEYS // 2, tn), tok),
                  pl.BlockSpec((tn, d), lambda t, e: (t, 0)),
                  pl.BlockSpec((1, 1, d), lambda t, e: (t // per_b, 0, 0)),
                  pl.BlockSpec((1, d), lambda t, e: (0, 0))],
        out_specs=pl.BlockSpec((tn, d), lambda t, e: (t, 0)),
        out_shape=jax.ShapeDtypeStruct((n, d), F32),
        scratch_shapes=[pltpu.VMEM((d, tn), F32),
                        pltpu.VMEM((eb, tn), F32),
                        pltpu.VMEM((eb, tn), BF16),
                        pltpu.VMEM((eb // 2, tn), I32)],
        compiler_params=_params(("arbitrary", "arbitrary")),
        name="peer_dense",
    )(ht, u, vt, rank, by_rows(cnt), by_rows(a), b, x2, gate, fgain)


def _peer(x2, shift, scale, gate, w_q, sub_keys, u_tab, v_tab, fgain, seq, final):
    tn = min(512, seq)
    ht, rank, cnt, a, b = _peer_route(x2, shift, scale, w_q.astype(BF16), sub_keys.astype(BF16), seq,
                                      min(256, seq))
    return _peer_dense(ht, u_tab.astype(BF16), v_tab.T.astype(BF16), rank, cnt, a, b, x2, gate, fgain,
                       seq, tn, P_EXPERT_BLOCK, final)


def kernel(x, c, a_w_in, a_kv_norm, a_w_uk, a_w_uv, a_w_out, b_w_in, b_conv, b_a_log, b_dt_bias,
           b_out_norm, b_w_out, p_w_q, p_sub_keys, p_u, p_v, ada_w, ada_b, final_norm):
    batch, seq, d = x.shape
    n = batch * seq
    depth = ada_w.shape[0]
    x2 = x.reshape(n, d)
    mod = _adaln(c, ada_w, ada_b).reshape(depth, batch, 6, 1, d)
    fgain = final_norm.reshape(1, d)
    tm = min(512, seq)
    for layer in range(depth):
        sh_m, sc_m, g_m, sh_f, sc_f, g_f = (mod[layer, :, k] for k in range(6))
        j = layer // 2
        if layer % 2 == 0:
            w_in = a_w_in[j]
            hr = A_HEADS * A_HEAD_DIM
            c0, c1, c2 = hr + A_KV_RANK, hr + A_KV_RANK + 512, hr + A_KV_RANK + 512 + A_IDX_DIM + A_IDX_HEADS
            w_a = jnp.concatenate([w_in[:, :hr], w_in[:, c0:c1], w_in[:, hr:c0], w_in[:, c1:c2],
                                   jnp.zeros((d, 1920 - c2), w_in.dtype)], axis=1).astype(BF16)
            proj = _norm_mod_matmul(x2, sh_m, sc_m, w_a, BF16, seq, tm, 1920)
            x2 = _dsa(proj, x2, g_m, a_kv_norm[j].reshape(1, -1),
                      a_w_uk[j].transpose(1, 2, 0).astype(BF16), a_w_uv[j].transpose(1, 0, 2).astype(BF16),
                      a_w_out[j].astype(BF16), batch, seq)
        else:
            w_in = b_w_in[j]
            nqkvz = 2 * B_K_HEADS * B_HEAD_DIM + 2 * B_V_HEADS * B_HEAD_DIM
            w_g = jnp.concatenate([w_in[:, nqkvz:], jnp.zeros((d, LANES - 2 * B_V_HEADS), w_in.dtype)], axis=1)
            qkvz = _norm_mod_matmul(x2, sh_m, sc_m, w_in[:, :nqkvz].astype(BF16), BF16, seq, tm, 1024)
            gates = _norm_mod_matmul(x2, sh_m, sc_m, w_g.astype(BF16), F32, seq, tm, LANES)
            gnat, gtr = _gdn_gates(gates, b_a_log[j], b_dt_bias[j], batch, seq)
            onorm = _gdn(qkvz, b_conv[j].reshape(B_CONV, -1), gnat, gtr, b_out_norm[j].reshape(1, -1), batch, seq)
            x2 = _matmul_residual(onorm, b_w_out[j].astype(BF16), x2, g_m, seq, tm)
        x2 = _peer(x2, sh_f, sc_f, g_f, p_w_q[layer], p_sub_keys[layer], p_u[layer], p_v[layer], fgain, seq,
                   final=(layer == depth - 1))
    return x2.reshape(batch, seq, d)
```

```python
import functools

import jax
import jax.numpy as jnp
from jax import lax
from jax.experimental import pallas as pl
from jax.experimental.pallas import tpu as pltpu

F32 = jnp.float32
BF16 = jnp.bfloat16
I32 = jnp.int32
HIGHEST = lax.Precision.HIGHEST

RMS_EPS = 1e-6
CHUNK = 64
LANES = 128
SUBLANES = 8
NEG_BIG = -1e30
SOFTMAX_MIN_SUM = 1e-30

A_HEADS = 16
A_HEAD_DIM = 64
A_KV_RANK = 256
A_IDX_HEADS = 8
A_IDX_DIM = 64
A_TOPK_MAX = 256
A_QBLOCK = 128
A_KTILE = 256

B_K_HEADS = 8
B_V_HEADS = 16
B_HEAD_DIM = 128
B_CONV = 4
B_SUPER = 256
B_GROUPS_IN_FLIGHT = 2

P_HEADS = 8
P_N_KEYS = 128
P_HALF = 128
P_TOPK = 16
P_TOKEN_TILE = 512
P_EXPERT_BLOCK = 1024
P_SUB_ROWS = 4
P_SLAB_GROUP = 4

INT_MIN = int(jnp.iinfo(jnp.int32).min)

_NT = (((1,), (1,)), ((), ()))
_TN = (((0,), (0,)), ((), ()))


def _vmem_limit(mib):
    return pltpu.CompilerParams(vmem_limit_bytes=mib * 1024 * 1024)


def _params(sem, mib=48):
    return pltpu.CompilerParams(dimension_semantics=sem, vmem_limit_bytes=mib * 1024 * 1024)


def _dot(a, b):
    return jnp.dot(a, b, preferred_element_type=F32)


def _dot_nt(a, b):
    return lax.dot_general(a, b, _NT, preferred_element_type=F32)


def _split_bf16(a):
    hi = a.astype(BF16)
    lo = (a - hi.astype(F32)).astype(BF16)
    return hi, lo


def _dot3(a, b):
    ah, al = _split_bf16(a)
    bh, bl = _split_bf16(b)
    return _dot(ah, bh) + (_dot(ah, bl) + _dot(al, bh))


def _rms(x):
    return x * lax.rsqrt(jnp.mean(x * x, axis=-1, keepdims=True) + RMS_EPS)


def _adaln_kernel(c_ref, w_ref, b_ref, o_ref):
    c = c_ref[...]
    cond = c * jax.nn.sigmoid(c)
    o_ref[0] = jnp.dot(cond, w_ref[0], preferred_element_type=F32, precision=HIGHEST) + b_ref[0]


def _adaln(c, ada_w, ada_b):
    depth, d, d6 = ada_w.shape
    b = c.shape[0]
    tn = 1536
    return pl.pallas_call(
        _adaln_kernel,
        grid=(depth, d6 // tn),
        in_specs=[pl.BlockSpec((b, d), lambda l, j: (0, 0)),
                  pl.BlockSpec((1, d, tn), lambda l, j: (l, 0, j)),
                  pl.BlockSpec((1, 1, tn), lambda l, j: (l, 0, j))],
        out_specs=pl.BlockSpec((1, b, tn), lambda l, j: (l, 0, j)),
        out_shape=jax.ShapeDtypeStruct((depth, b, d6), F32),
        compiler_params=_params(("arbitrary", "arbitrary")),
        name="adaln",
    )(c, ada_w, ada_b.reshape(depth, 1, d6))


def _nmm_kernel(x_ref, sh_ref, sc_ref, w_ref, o_ref, h_ref):
    @pl.when(pl.program_id(1) == 0)
    def _():
        h = _rms(x_ref[...]) * (1.0 + sc_ref[0]) + sh_ref[0]
        h_ref[...] = h.astype(BF16)

    o_ref[...] = _dot(h_ref[...], w_ref[...]).astype(o_ref.dtype)


def _norm_mod_matmul(x2, shift, scale, w, out_dtype, seq, tm, tn):
    n, d = x2.shape
    n_out = w.shape[1]
    per_b = seq // tm
    return pl.pallas_call(
        _nmm_kernel,
        grid=(n // tm, n_out // tn),
        in_specs=[pl.BlockSpec((tm, d), lambda i, j: (i, 0)),
                  pl.BlockSpec((1, 1, d), lambda i, j: (i // per_b, 0, 0)),
                  pl.BlockSpec((1, 1, d), lambda i, j: (i // per_b, 0, 0)),
                  pl.BlockSpec((d, tn), lambda i, j: (0, j))],
        out_specs=pl.BlockSpec((tm, tn), lambda i, j: (i, j)),
        out_shape=jax.ShapeDtypeStruct((n, n_out), out_dtype),
        scratch_shapes=[pltpu.VMEM((tm, d), BF16)],
        compiler_params=_params(("arbitrary", "arbitrary")),
        name="norm_mod_matmul",
    )(x2, shift, scale, w)


def _mmres_kernel(a_ref, w_ref, x_ref, g_ref, o_ref):
    o_ref[...] = x_ref[...] + g_ref[0] * _dot(a_ref[...], w_ref[...])


def _matmul_residual(a, w, x2, gate, seq, tm):
    n, k = a.shape
    d = w.shape[1]
    per_b = seq // tm
    return pl.pallas_call(
        _mmres_kernel,
        grid=(n // tm,),
        in_specs=[pl.BlockSpec((tm, k), lambda i: (i, 0)),
                  pl.BlockSpec((k, d), lambda i: (0, 0)),
                  pl.BlockSpec((tm, d), lambda i: (i, 0)),
                  pl.BlockSpec((1, 1, d), lambda i: (i // per_b, 0, 0))],
        out_specs=pl.BlockSpec((tm, d), lambda i: (i, 0)),
        out_shape=jax.ShapeDtypeStruct((n, d), F32),
        compiler_params=_params(("arbitrary",)),
        name="matmul_residual",
    )(a, w, x2, gate)


def _dsa_kernel(q_ref, qidx_ref, kwq_ref, ckv_ref, kw_ref, gain_ref, wuk_ref, wuv_ref, wout_ref,
                x_ref, gm_ref, o_ref,
                ckvn_ref, kn_ref, keys_ref, bound_ref, ql_ref, m_ref, l_ref, acc_ref, *, n_sel):
    qi = pl.program_id(1)
    tq = A_QBLOCK
    n_kt = qi + 1
    seq = keys_ref.shape[0]

    @pl.when(qi == 0)
    def _():
        cn = (_rms(ckv_ref[...].astype(F32)) * gain_ref[...]).astype(BF16)
        ckvn_ref[...] = cn
        cf = cn.astype(F32)
        kn2 = jnp.max(jnp.sum(cf * cf, axis=1, keepdims=True), axis=0, keepdims=True)
        kn_ref[...] = jnp.broadcast_to(kn2, kn_ref.shape)

    krow = lax.broadcasted_iota(I32, (LANES, tq), 0)
    qlane = lax.broadcasted_iota(I32, (LANES, tq), 1)
    qh = [qidx_ref[:, h * A_IDX_DIM:(h + 1) * A_IDX_DIM] for h in range(A_IDX_HEADS)]
    w_t = kwq_ref[...].astype(F32).T * ((A_IDX_HEADS ** -0.5) * (A_IDX_DIM ** -0.5))
    wrow = [w_t[A_IDX_DIM + h:A_IDX_DIM + h + 1, :] for h in range(A_IDX_HEADS)]
    diag_inadm = (krow >= CHUNK) & (qlane < CHUNK)

    n_kt2 = (n_kt + 1) // 2

    def score_pair(t, carry):
        kts = [2 * t, 2 * t + 1]
        offs = [pl.multiple_of(kt * LANES, LANES) for kt in kts]
        kt_k = [kw_ref[pl.ds(off, LANES), 0:A_IDX_DIM] for off in offs]
        sc = [jnp.zeros((LANES, tq), F32) for _ in kts]
        for h in range(A_IDX_HEADS):
            for j in range(2):
                sc[j] = sc[j] + wrow[h] * jnp.maximum(_dot_nt(kt_k[j], qh[h]), 0.0)
        for j in range(2):
            bits = lax.bitcast_convert_type(sc[j], I32)
            key = jnp.where(bits < 0, bits ^ jnp.int32(0x7FFFFFFF), bits)
            inadm = (diag_inadm & (kts[j] == qi)) | (kts[j] > qi)
            keys_ref[pl.ds(offs[j], LANES), :] = jnp.where(inadm, jnp.int32(INT_MIN), key)
        return carry

    lax.fori_loop(0, n_kt2, score_pair, 0)

    kidx = lax.broadcasted_iota(I32, (A_KTILE, tq), 0)

    def count(pred):
        def body(t, acc):
            off = pl.multiple_of(t * A_KTILE, A_KTILE)
            return acc + jnp.where(pred(keys_ref[pl.ds(off, A_KTILE), :], off), 1.0, 0.0)
        acc = lax.fori_loop(0, n_kt2, body, jnp.zeros((A_KTILE, tq), F32))
        return jnp.sum(acc, axis=0, keepdims=True)

    def count_ge(cand):
        cb = jnp.broadcast_to(cand, (A_KTILE, tq))
        return count(lambda k, off: k >= cb)

    kf = float(n_sel)
    zero = jnp.zeros((1, tq), I32)
    ans = jnp.where(count_ge(zero) >= kf, zero, jnp.int32(INT_MIN))

    def bit_body(b, ans):
        cand = ans + jnp.left_shift(jnp.int32(1), 30 - b)
        return jnp.where(count_ge(cand) >= kf, cand, ans)

    ans = lax.fori_loop(0, 31, bit_body, ans)
    thr = jnp.maximum(ans, jnp.int32(INT_MIN + 1))
    thr_k = jnp.broadcast_to(thr, (A_KTILE, tq))
    n_gt = count(lambda k, off: k > thr_k)
    n_ge = count(lambda k, off: k >= thr_k)
    need = kf - n_gt

    bound_ref[...] = jnp.full(bound_ref.shape, float(2 * seq), F32)

    @pl.when(jnp.max(n_ge) > kf)
    def _():
        def eq_before(bound):
            bb = jnp.broadcast_to(bound, (A_KTILE, tq))
            return count(lambda k, off: (k == thr_k) & ((off + kidx).astype(F32) < bb))

        nbits = max(1, (seq - 1).bit_length())

        def jb(b, bound):
            cand = bound + jnp.left_shift(jnp.int32(1), nbits - b).astype(F32)
            return jnp.where(eq_before(cand) <= need, cand, bound)

        bound_ref[...] = jnp.broadcast_to(lax.fori_loop(0, nbits + 1, jb, jnp.zeros((1, tq), F32)), bound_ref.shape)

    bound_b = jnp.broadcast_to(bound_ref[0:1, :], (A_KTILE, tq))

    qn2 = jnp.zeros((tq, 1), F32)
    for h in range(A_HEADS):
        qlat = _dot(q_ref[:, h * A_HEAD_DIM:(h + 1) * A_HEAD_DIM], wuk_ref[h]) * (A_HEAD_DIM ** -0.5)
        qlb = qlat.astype(BF16)
        ql_ref[h * tq:(h + 1) * tq, :] = qlb
        qf = qlb.astype(F32)
        qn2 = jnp.maximum(qn2, jnp.sum(qf * qf, axis=1, keepdims=True))

    def selected(key, off):
        return (key > thr_k) | ((key == thr_k) & ((off + kidx).astype(F32) < bound_b))

    logit_cap = jnp.sqrt(qn2 * kn_ref[:, 0:1]) * (1.0 + 2.0 ** -7) + 2.0 ** -20
    cap_b = jnp.broadcast_to(jnp.broadcast_to(logit_cap, (tq, LANES)).T[0:1, :], (A_KTILE, tq))
    l_ref[...] = jnp.zeros(l_ref.shape, F32)
    acc_ref[...] = jnp.zeros(acc_ref.shape, F32)

    def att_tile_capped(t, carry):
        off = pl.multiple_of(t * A_KTILE, A_KTILE)
        kv = ckvn_ref[pl.ds(off, A_KTILE), :]
        bias = jnp.where(selected(keys_ref[pl.ds(off, A_KTILE), :], off), -cap_b, NEG_BIG).T
        p = jnp.exp(_dot_nt(ql_ref[...], kv).reshape(A_HEADS, tq, A_KTILE) + bias[None])
        l_ref[...] += jnp.sum(p, axis=2, keepdims=True).reshape(A_HEADS * tq, 1)
        acc_ref[...] += _dot(p.reshape(A_HEADS * tq, A_KTILE).astype(BF16), kv)
        return carry

    lax.fori_loop(0, n_kt2, att_tile_capped, 0)

    @pl.when(jnp.logical_not(jnp.min(l_ref[...]) > SOFTMAX_MIN_SUM))
    def _():
        m_ref[...] = jnp.full(m_ref.shape, NEG_BIG, F32)
        l_ref[...] = jnp.zeros(l_ref.shape, F32)
        acc_ref[...] = jnp.zeros(acc_ref.shape, F32)

        def att_tile(t, carry):
            off = pl.multiple_of(t * A_KTILE, A_KTILE)
            kv = ckvn_ref[pl.ds(off, A_KTILE), :]
            bias = jnp.where(selected(keys_ref[pl.ds(off, A_KTILE), :], off), 0.0, NEG_BIG).T
            s = _dot_nt(ql_ref[...], kv).reshape(A_HEADS, tq, A_KTILE) + bias[None]
            m_old = m_ref[...].reshape(A_HEADS, tq, 1)
            m_new = jnp.maximum(m_old, jnp.max(s, axis=2, keepdims=True))
            p = jnp.exp(s - m_new)
            alpha = jnp.exp(m_old - m_new)
            l_ref[...] = (alpha * l_ref[...].reshape(A_HEADS, tq, 1)
                          + jnp.sum(p, axis=2, keepdims=True)).reshape(A_HEADS * tq, 1)
            m_ref[...] = m_new.reshape(A_HEADS * tq, 1)
            pv = _dot(p.reshape(A_HEADS * tq, A_KTILE).astype(BF16), kv)
            acc_ref[...] = alpha.reshape(A_HEADS * tq, 1) * acc_ref[...] + pv
            return carry

        lax.fori_loop(0, n_kt2, att_tile, 0)

    outs = []
    for h in range(A_HEADS):
        rs = slice(h * tq, (h + 1) * tq)
        o_lat = acc_ref[rs, :] / l_ref[rs, :]
        outs.append(_dot(o_lat.astype(BF16), wuv_ref[h]))
    att = jnp.concatenate(outs, axis=1)
    o_ref[...] = x_ref[...] + gm_ref[0] * _dot(att.astype(BF16), wout_ref[...])


def _dsa(proj, x2, gate, kv_gain, wuk_t, wuv, wout, batch, seq):
    n, d = x2.shape
    nq = seq // A_QBLOCK
    hr = A_HEADS * A_HEAD_DIM
    n_sel = min(A_TOPK_MAX, seq // 4)
    return pl.pallas_call(
        functools.partial(_dsa_kernel, n_sel=n_sel),
        grid=(batch, nq),
        in_specs=[pl.BlockSpec((A_QBLOCK, hr), lambda b, i: (b * nq + i, 0)),
                  pl.BlockSpec((A_QBLOCK, 512), lambda b, i: (b * nq + i, 2)),
                  pl.BlockSpec((A_QBLOCK, LANES), lambda b, i: (b * nq + i, 14)),
                  pl.BlockSpec((seq, A_KV_RANK), lambda b, i: (b, 6)),
                  pl.BlockSpec((seq, LANES), lambda b, i: (b, 14)),
                  pl.BlockSpec((1, A_KV_RANK), lambda b, i: (0, 0)),
                  pl.BlockSpec((A_HEADS, A_HEAD_DIM, A_KV_RANK), lambda b, i: (0, 0, 0)),
                  pl.BlockSpec((A_HEADS, A_KV_RANK, A_HEAD_DIM), lambda b, i: (0, 0, 0)),
                  pl.BlockSpec((hr, d), lambda b, i: (0, 0)),
                  pl.BlockSpec((A_QBLOCK, d), lambda b, i: (b * nq + i, 0)),
                  pl.BlockSpec((1, 1, d), lambda b, i: (b, 0, 0))],
        out_specs=pl.BlockSpec((A_QBLOCK, d), lambda b, i: (b * nq + i, 0)),
        out_shape=jax.ShapeDtypeStruct((n, d), F32),
        scratch_shapes=[pltpu.VMEM((seq, A_KV_RANK), BF16),
                        pltpu.VMEM((1, LANES), F32),
                        pltpu.VMEM((seq, A_QBLOCK), I32),
                        pltpu.VMEM((SUBLANES, A_QBLOCK), F32),
                        pltpu.VMEM((A_HEADS * A_QBLOCK, A_KV_RANK), BF16),
                        pltpu.VMEM((A_HEADS * A_QBLOCK, 1), F32),
                        pltpu.VMEM((A_HEADS * A_QBLOCK, 1), F32),
                        pltpu.VMEM((A_HEADS * A_QBLOCK, A_KV_RANK), F32)],
        compiler_params=_params(("arbitrary", "arbitrary")),
        name="dsa_attention",
    )(proj, proj, proj, proj, proj, kv_gain, wuk_t, wuv, wout, x2, gate)


def _gdn_gate_kernel(g_ref, alog_ref, dtb_ref, nat_ref, tr_ref):
    g = g_ref[...]
    col = lax.broadcasted_iota(I32, g.shape, 1)
    beta = jax.nn.sigmoid(g)
    z = g + dtb_ref[...]
    softplus = jnp.maximum(z, 0.0) + jnp.log1p(jnp.exp(-jnp.abs(z)))
    gd = -jnp.exp(alog_ref[...]) * softplus
    r = lax.broadcasted_iota(I32, (B_SUPER, B_SUPER), 0)
    c = lax.broadcasted_iota(I32, (B_SUPER, B_SUPER), 1)
    same = (r >> 6) == (c >> 6)
    tril = jnp.where(same & (c <= r), 1.0, 0.0).astype(F32)
    gc = jnp.dot(tril, gd, preferred_element_type=F32, precision=HIGHEST)
    nat = jnp.where(col < B_V_HEADS, beta, gc)
    nat_ref[...] = nat
    tr_ref[0, 0] = nat.T


def _gdn_gates(gates, a_log, dt_bias, batch, seq):
    n = gates.shape[0]
    ns = seq // B_SUPER
    pad = lambda v: jnp.zeros((1, LANES), F32).at[0, B_V_HEADS:2 * B_V_HEADS].set(v.astype(F32))
    return pl.pallas_call(
        _gdn_gate_kernel,
        grid=(batch, ns),
        in_specs=[pl.BlockSpec((B_SUPER, LANES), lambda b, s: (b * ns + s, 0)),
                  pl.BlockSpec((1, LANES), lambda b, s: (0, 0)),
                  pl.BlockSpec((1, LANES), lambda b, s: (0, 0))],
        out_specs=[pl.BlockSpec((B_SUPER, LANES), lambda b, s: (b * ns + s, 0)),
                   pl.BlockSpec((1, 1, LANES, B_SUPER), lambda b, s: (b, s, 0, 0))],
        out_shape=[jax.ShapeDtypeStruct((n, LANES), F32),
                   jax.ShapeDtypeStruct((batch, ns, LANES, B_SUPER), F32)],
        compiler_params=_params(("arbitrary", "arbitrary")),
        name="gdn_gates",
    )(gates, pad(a_log), pad(dt_bias))


def _conv_silu(raw_ref, w_ref, pad_ref):
    t = raw_ref.shape[0]
    pad_ref[0:8, :] = jnp.zeros((8, LANES), F32)
    pad_ref[8:8 + t, :] = raw_ref[...].astype(F32)
    w = w_ref[...]
    y = w[3:4] * pad_ref[8:8 + t, :]
    for j in range(B_CONV - 1):
        y = y + w[j:j + 1] * pad_ref[5 + j:5 + j + t, :]
    return y * jax.nn.sigmoid(y)


def _l2n(y):
    return y * lax.rsqrt(jnp.sum(y * y, axis=-1, keepdims=True) + RMS_EPS)


def _gdn_kernel(q_ref, k_ref, v0_ref, v1_ref, z_ref, wq_ref, wk_ref, wv0_ref, wv1_ref,
                gnat_ref, gtr_ref, gain_ref, o_ref,
                pad_ref, qn_ref, kn_ref, vs_ref, u_ref, wqg_ref, kg_ref, at_ref, el_ref, oo_ref):
    hk = pl.program_id(1)
    seq = q_ref.shape[0]
    ns = seq // B_SUPER
    nc = seq // CHUNK

    qn_ref[...] = _l2n(_conv_silu(q_ref, wq_ref, pad_ref)) * (B_HEAD_DIM ** -0.5)
    kn_ref[...] = _l2n(_conv_silu(k_ref, wk_ref, pad_ref))
    vs_ref[0] = _conv_silu(v0_ref, wv0_ref, pad_ref)
    vs_ref[1] = _conv_silu(v1_ref, wv1_ref, pad_ref)

    r = lax.broadcasted_iota(I32, (B_SUPER, B_SUPER), 0)
    c = lax.broadcasted_iota(I32, (B_SUPER, B_SUPER), 1)
    same = (r >> 6) == (c >> 6)
    tril = same & (c <= r)
    strict = same & (c < r)
    is_last = c == (r | (CHUNK - 1))
    eye = jnp.where(r == c, 1.0, 0.0).astype(F32)
    lane = lax.broadcasted_iota(I32, (B_SUPER, LANES), 1)

    def precompute(s2, carry):
        chains = []
        for j in range(B_GROUPS_IN_FLIGHT):
            s = s2 * B_GROUPS_IN_FLIGHT + j
            rows = pl.ds(pl.multiple_of(s * B_SUPER, B_SUPER), B_SUPER)
            kn = kn_ref[rows, :]
            qn = qn_ref[rows, :]
            knb = kn.astype(BF16)
            kk = _dot_nt(knb, knb)
            qk = _dot_nt(qn.astype(BF16), knb)
            gnat = gnat_ref[rows, :]
            for e in range(2):
                hv = 2 * hk + e
                bcol = jnp.sum(jnp.where(lane == hv, gnat, 0.0), axis=1, keepdims=True)
                gcol = jnp.sum(jnp.where(lane == hv + B_V_HEADS, gnat, 0.0), axis=1, keepdims=True)
                grow = gtr_ref[0, s, pl.ds(hv + B_V_HEADS, 1), :]
                grow_b = jnp.broadcast_to(grow, (B_SUPER, B_SUPER))
                dec = jnp.where(tril, jnp.exp(gcol - grow_b), 0.0)
                a = jnp.where(strict, bcol * kk * dec, 0.0)
                eg = jnp.exp(gcol)
                glast = jnp.sum(jnp.where(is_last, grow_b, 0.0), axis=1, keepdims=True)
                qg = (qn * eg).astype(BF16)
                kg_ref[e, rows, :] = (kn * jnp.exp(glast - gcol)).astype(BF16)
                el_ref[e, rows, :] = jnp.exp(glast)
                attn = jnp.where(tril, qk * dec, 0.0).astype(BF16)
                for c in range(B_SUPER // CHUNK):
                    crows = slice(c * CHUNK, (c + 1) * CHUNK)
                    at_ref[e, pl.ds(pl.multiple_of(s * B_SUPER + c * CHUNK, CHUNK), CHUNK), :] = attn[crows, crows]
                    wqg_ref[e, pl.ds(pl.multiple_of((2 * s) * B_SUPER + (2 * c + 1) * CHUNK, CHUNK), CHUNK), :] = qg[crows]
                chains.append(dict(e=e, s=s, rows=rows, a=a, vb=vs_ref[e, rows, :] * bcol, kb=kn * (bcol * eg)))
        for ch in chains:
            ch["pw"] = -ch["a"]
            ch["tm"] = eye + ch["pw"]
        for _ in range(5):
            for ch in chains:
                pwb = ch["pw"].astype(BF16)
                ch["pw"] = _dot(pwb, pwb)
            for ch in chains:
                ch["tm"] = ch["tm"] + _dot(ch["tm"].astype(BF16), ch["pw"].astype(BF16))
        for ch in chains:
            ch["resid"] = eye - _dot3(eye + ch["a"], ch["tm"])
        for ch in chains:
            ch["tm"] = ch["tm"] + _dot(ch["tm"].astype(BF16), ch["resid"].astype(BF16))
        for ch in chains:
            u_ref[ch["e"], ch["rows"], :] = _dot3(ch["tm"], ch["vb"])
            w = _dot3(ch["tm"], ch["kb"]).astype(BF16)
            for c in range(B_SUPER // CHUNK):
                wrows = pl.ds(pl.multiple_of((2 * ch["s"]) * B_SUPER + (2 * c) * CHUNK, CHUNK), CHUNK)
                wqg_ref[ch["e"], wrows, :] = w[c * CHUNK:(c + 1) * CHUNK]
        return carry

    lax.fori_loop(0, ns // B_GROUPS_IN_FLIGHT, precompute, 0)

    def step(n, states):
        rows = pl.ds(pl.multiple_of(n * CHUNK, CHUNK), CHUNK)
        rows2 = pl.ds(pl.multiple_of(n * 2 * CHUNK, 2 * CHUNK), 2 * CHUNK)
        heads = range(2)
        sb = [states[e].astype(BF16) for e in heads]
        ws = [_dot(wqg_ref[e, rows2, :], sb[e]) for e in heads]
        vb = [(u_ref[e, rows, :] - ws[e][:CHUNK]).astype(BF16) for e in heads]
        for e in heads:
            oo_ref[e, rows, :] = ws[e][CHUNK:] + _dot(at_ref[e, rows, :], vb[e])
        return tuple(states[e] * el_ref[e, pl.ds(n * CHUNK, 1), :]
                     + lax.dot_general(kg_ref[e, rows, :], vb[e], _TN, preferred_element_type=F32)
                     for e in heads)

    zero = jnp.zeros((B_HEAD_DIM, B_HEAD_DIM), F32)
    lax.fori_loop(0, nc, step, (zero, zero))

    for e in range(2):
        z = z_ref[:, e * B_HEAD_DIM:(e + 1) * B_HEAD_DIM].astype(F32)
        y = _rms(oo_ref[e]) * gain_ref[...] * (z * jax.nn.sigmoid(z))
        o_ref[:, e * B_HEAD_DIM:(e + 1) * B_HEAD_DIM] = y.astype(o_ref.dtype)


def _gdn(qkvz, conv_w, gnat, gtr, out_gain, batch, seq):
    n = qkvz.shape[0]
    hd = B_HEAD_DIM
    kh = B_K_HEADS
    ns = seq // B_SUPER
    col = lambda f: (lambda b, h: (b, f(h)))
    wcol = lambda f: (lambda b, h: (0, f(h)))
    return pl.pallas_call(
        _gdn_kernel,
        grid=(batch, kh),
        in_specs=[pl.BlockSpec((seq, hd), col(lambda h: h)),
                  pl.BlockSpec((seq, hd), col(lambda h: kh + h)),
                  pl.BlockSpec((seq, hd), col(lambda h: 2 * kh + 2 * h)),
                  pl.BlockSpec((seq, hd), col(lambda h: 2 * kh + 2 * h + 1)),
                  pl.BlockSpec((seq, 2 * hd), col(lambda h: 2 * kh + h)),
                  pl.BlockSpec((B_CONV, hd), wcol(lambda h: h)),
                  pl.BlockSpec((B_CONV, hd), wcol(lambda h: kh + h)),
                  pl.BlockSpec((B_CONV, hd), wcol(lambda h: 2 * kh + 2 * h)),
                  pl.BlockSpec((B_CONV, hd), wcol(lambda h: 2 * kh + 2 * h + 1)),
                  pl.BlockSpec((seq, LANES), lambda b, h: (b, 0)),
                  pl.BlockSpec((1, ns, LANES, B_SUPER), lambda b, h: (b, 0, 0, 0)),
                  pl.BlockSpec((1, hd), lambda b, h: (0, 0))],
        out_specs=pl.BlockSpec((seq, 2 * hd), lambda b, h: (b, h)),
        out_shape=jax.ShapeDtypeStruct((n, B_V_HEADS * hd), BF16),
        scratch_shapes=[pltpu.VMEM((seq + 8, hd), F32),
                        pltpu.VMEM((seq, hd), F32),
                        pltpu.VMEM((seq, hd), F32),
                        pltpu.VMEM((2, seq, hd), F32),
                        pltpu.VMEM((2, seq, hd), F32),
                        pltpu.VMEM((2, 2 * seq, hd), BF16),
                        pltpu.VMEM((2, seq, hd), BF16),
                        pltpu.VMEM((2, seq, CHUNK), BF16),
                        pltpu.VMEM((2, seq, 1), F32),
                        pltpu.VMEM((2, seq, hd), F32)],
        compiler_params=_params(("arbitrary", "arbitrary")),
        name="gated_deltanet",
    )(qkvz, qkvz, qkvz, qkvz, qkvz, conv_w, conv_w, conv_w, conv_w, gnat, gtr, out_gain)


_CAND_W = [P_TOPK // (r + 1) for r in range(P_TOPK)]
_CAND_N = sum(_CAND_W)
_CAND_PAD = -(-_CAND_N // 8) * 8


def _top16_ranks(s, iota_k=None):
    rank = jnp.full(s.shape, float(P_TOPK), F32)
    vals = []
    for r in range(P_TOPK):
        m = jnp.max(s, axis=0, keepdims=True)
        hit = s == m
        if iota_k is not None:
            hit = iota_k == jnp.min(jnp.where(hit, iota_k, P_N_KEYS), axis=0, keepdims=True)
        rank = jnp.where(hit, float(r), rank)
        s = jnp.where(hit, -jnp.inf, s)
        vals.append(m)
    return vals, rank


def _route_head(s1, s2, iota_k):
    tn = s1.shape[1]
    v1, rank1 = _top16_ranks(s1, iota_k)
    v2, rank2 = _top16_ranks(s2, iota_k)
    v2s = jnp.concatenate(v2, axis=0)
    ea = [jnp.exp(v1[r] - v1[0]) for r in range(P_TOPK)]
    eb = jnp.exp(v2s - v2[0])
    neg_pad = jnp.full((_CAND_PAD - _CAND_N, tn), -jnp.inf, F32)
    iota_c = lax.broadcasted_iota(I32, (_CAND_PAD, tn), 0)
    cand = jnp.concatenate([v1[r] + v2s[0:_CAND_W[r]] for r in range(P_TOPK)] + [neg_pad], axis=0)
    gate = jnp.concatenate([ea[r] * eb[0:_CAND_W[r]] for r in range(P_TOPK)] + [jnp.zeros_like(neg_pad)], axis=0)
    sel = jnp.zeros((_CAND_PAD, tn), F32)
    work = cand
    for _ in range(P_TOPK):
        m = jnp.max(work, axis=0, keepdims=True)
        idx = jnp.min(jnp.where(work == m, iota_c, _CAND_PAD), axis=0, keepdims=True)
        hit = iota_c == idx
        sel = jnp.where(hit, 1.0, sel)
        work = jnp.where(hit, -jnp.inf, work)
    inv_z = 1.0 / jnp.sum(sel * gate, axis=0, keepdims=True)
    cnt_t = jnp.zeros(s1.shape, F32)
    off = 0
    for r in range(P_TOPK):
        n_r = jnp.sum(sel[off:off + _CAND_W[r]], axis=0, keepdims=True)
        off += _CAND_W[r]
        cnt_t = jnp.where(rank1 == float(r), n_r, cnt_t)
    a_t = jnp.where(rank1 < float(P_TOPK), jnp.exp(s1 - v1[0]), 0.0)
    b_t = jnp.where(rank2 < float(P_TOPK), jnp.exp(s2 - v2[0]) * inv_z, 0.0)
    n_ranked = jnp.sum(jnp.where(rank1 < float(P_TOPK), 1.0, 0.0) + jnp.where(rank2 < float(P_TOPK), 1.0, 0.0),
                       axis=0, keepdims=True)
    return rank2, cnt_t, a_t, b_t, n_ranked


def _bf16_pair_words(lo, hi):
    bits = lambda v: lax.bitcast_convert_type(v.astype(BF16).astype(F32), I32)
    return bits(hi) | lax.shift_right_logical(bits(lo), 16)


def _peer_route_kernel(x_ref, sh_ref, sc_ref, wq_ref, sk_ref, ht_ref, rank_ref, cnt_ref, a_ref, b_ref):
    tn = x_ref.shape[0]
    h = _rms(x_ref[...]) * (1.0 + sc_ref[0]) + sh_ref[0]
    ht_ref[...] = h.T.astype(BF16)
    qb = _dot(h.astype(BF16), wq_ref[...]).astype(BF16)
    iota_k = lax.broadcasted_iota(I32, (P_N_KEYS, LANES), 0)
    half = P_N_KEYS // 2
    for hd in range(P_HEADS):
        base = hd * 2 * P_HALF
        s1 = _dot_nt(sk_ref[hd, 0], qb[:, base:base + P_HALF])
        s2 = _dot_nt(sk_ref[hd, 1], qb[:, base + P_HALF:base + 2 * P_HALF])

        def emit(index_ties):
            most_ranked = jnp.zeros((), F32)
            for t in range(tn // LANES):
                lanes = slice(t * LANES, (t + 1) * LANES)
                rank2, cnt_t, a_t, b_t, n_ranked = _route_head(s1[:, lanes], s2[:, lanes],
                                                               iota_k if index_ties else None)
                rank_ref[hd, :, lanes] = _bf16_pair_words(rank2[:half], rank2[half:])
                cnt_ref[hd, :, lanes] = _bf16_pair_words(cnt_t, cnt_t)
                a_ref[hd, :, lanes] = _bf16_pair_words(a_t, a_t)
                b_ref[hd, :, lanes] = _bf16_pair_words(b_t[:half], b_t[half:])
                most_ranked = jnp.maximum(most_ranked, jnp.max(n_ranked))
            return most_ranked

        most_ranked = emit(False)

        @pl.when(most_ranked > float(2 * P_TOPK))
        def _():
            emit(True)


def _peer_route(x2, shift, scale, wq, sub_keys, seq, tn):
    n, d = x2.shape
    per_b = seq // tn
    tok = lambda i: (0, 0, i)
    arr = jax.ShapeDtypeStruct((P_HEADS, P_N_KEYS, n), I32)
    arr_pair = jax.ShapeDtypeStruct((P_HEADS, P_N_KEYS // 2, n), I32)
    return pl.pallas_call(
        _peer_route_kernel,
        grid=(n // tn,),
        in_specs=[pl.BlockSpec((tn, d), lambda i: (i, 0)),
                  pl.BlockSpec((1, 1, d), lambda i: (i // per_b, 0, 0)),
                  pl.BlockSpec((1, 1, d), lambda i: (i // per_b, 0, 0)),
                  pl.BlockSpec(wq.shape, lambda i: (0, 0)),
                  pl.BlockSpec(sub_keys.shape, lambda i: (0, 0, 0, 0))],
        out_specs=[pl.BlockSpec((d, tn), lambda i: (0, i)),
                   pl.BlockSpec((P_HEADS, P_N_KEYS // 2, tn), tok),
                   pl.BlockSpec((P_HEADS, P_N_KEYS, tn), tok),
                   pl.BlockSpec((P_HEADS, P_N_KEYS, tn), tok),
                   pl.BlockSpec((P_HEADS, P_N_KEYS // 2, tn), tok)],
        out_shape=[jax.ShapeDtypeStruct((d, n), BF16), arr_pair, arr, arr, arr_pair],
        compiler_params=_params(("arbitrary",)),
        name="peer_route",
    )(x2, shift, scale, wq, sub_keys)


def _gelu(x):
    return 0.5 * x * (1.0 + lax.erf(x * (2.0 ** -0.5)))


def _peer_dense_kernel(ht_ref, u_ref, vt_ref, rank_ref, cnt_ref, a_ref, b_ref, x_ref, gf_ref, fn_ref,
                       o_ref, acc_ref, act_ref, p_ref, g_ref, *, final):
    e = pl.program_id(1)
    eb = u_ref.shape[0]
    n_i = eb // P_N_KEYS

    @pl.when(e == 0)
    def _():
        acc_ref[...] = jnp.zeros(acc_ref.shape, F32)

    tn = ht_ref.shape[1]
    half = P_N_KEYS // 2
    n_slab = half // SUBLANES
    zero = jnp.zeros((2 * SUBLANES, LANES), BF16)

    def words(ref, hd, rows, lanes):
        return pltpu.bitcast(ref[hd, rows, lanes], BF16)

    def row_tile(ref, hd, ii, lanes):
        return pltpu.bitcast(jnp.broadcast_to(ref[hd, 0, ii:ii + 1, lanes], (SUBLANES, LANES)), BF16)

    def gate_build(i_rows):
        for t in range(tn // LANES):
            lanes = slice(t * LANES, (t + 1) * LANES)
            for s0 in range(0, n_slab, P_SLAB_GROUP):
                slabs = range(s0, s0 + P_SLAB_GROUP)
                g = {(ii, s): zero for ii in i_rows for s in slabs}
                for hd in range(P_HEADS):
                    r = {s: words(rank_ref, hd, slice(s * SUBLANES, (s + 1) * SUBLANES), lanes) for s in slabs}
                    b = {s: words(b_ref, hd, slice(s * SUBLANES, (s + 1) * SUBLANES), lanes) for s in slabs}
                    for ii in i_rows:
                        cnt = row_tile(cnt_ref, hd, ii, lanes)
                        a = row_tile(a_ref, hd, ii, lanes)
                        for s in slabs:
                            g[ii, s] = g[ii, s] + a * jnp.where(r[s] < cnt, b[s], zero)
                for ii in i_rows:
                    for s in slabs:
                        wrows = slice(ii * half + s * SUBLANES, ii * half + (s + 1) * SUBLANES)
                        g_ref[wrows, lanes] = pltpu.bitcast(g[ii, s], I32)

    def apply_act(i_rows):
        for ii in i_rows:
            gw = g_ref[ii * half:(ii + 1) * half, :]
            lo = slice(ii * P_N_KEYS, ii * P_N_KEYS + half)
            hi = slice(ii * P_N_KEYS + half, (ii + 1) * P_N_KEYS)
            g_lo = lax.bitcast_convert_type(lax.shift_left(gw, 16), F32)
            g_hi = lax.bitcast_convert_type(gw & jnp.int32(-65536), F32)
            p_ref[lo, :] = (g_lo * _gelu(act_ref[lo, :])).astype(BF16)
            p_ref[hi, :] = (g_hi * _gelu(act_ref[hi, :])).astype(BF16)

    subs = [range(i0, i0 + P_SUB_ROWS) for i0 in range(0, n_i, P_SUB_ROWS)]
    rows_of = lambda sub: slice(sub[0] * P_N_KEYS, (sub[-1] + 1) * P_N_KEYS)
    gate_build(subs[0])
    for sub in subs:
        act_ref[rows_of(sub), :] = _dot(u_ref[rows_of(sub), :], ht_ref[...])
    for k, sub in enumerate(subs):
        apply_act(sub)
        if k + 1 < len(subs):
            gate_build(subs[k + 1])
        acc_ref[...] += _dot(vt_ref[:, rows_of(sub)], p_ref[rows_of(sub), :])

    @pl.when(e == pl.num_programs(1) - 1)
    def _():
        xn = x_ref[...] + gf_ref[0] * acc_ref[...].T
        if final:
            xn = _rms(xn) * fn_ref[...]
        o_ref[...] = xn


def _peer_dense(ht, u, vt, rank, cnt, a, b, x2, gate, fgain, seq, tn, eb, final):
    n, d = x2.shape
    n_e = u.shape[0]
    per_b = seq // tn
    tok = lambda t, e: (0, 0, t)
    n_i = eb // P_N_KEYS
    rows_of = lambda t, e: (0, e, 0, t)
    by_rows = lambda v: v.reshape(P_HEADS, P_N_KEYS // n_i, n_i, n)
    return pl.pallas_call(
        functools.partial(_peer_dense_kernel, final=final),
        grid=(n // tn, n_e // eb),
        in_specs=[pl.BlockSpec((d, tn), lambda t, e: (0, t)),
                  pl.BlockSpec((eb, d), lambda t, e: (e, 0)),
                  pl.BlockSpec((d, eb), lambda t, e: (0, e)),
                  pl.BlockSpec((P_HEADS, P_N_KEYS // 2, tn), tok),
                  pl.BlockSpec((P_HEADS, 1, n_i, tn), rows_of),
                  pl.BlockSpec((P_HEADS, 1, n_i, tn), rows_of),
                  pl.BlockSpec((P_HEADS, P_N_KEYS // 2, tn), tok),
                  pl.BlockSpec((tn, d), lambda t, e: (t, 0)),
                  pl.BlockSpec((1, 1, d), lambda t, e: (t // per_b, 0, 0)),
                  pl.BlockSpec((1, d), lambda t, e: (0, 0))],
        out_specs=pl.BlockSpec((tn, d), lambda t, e: (t, 0)),
        out_shape=jax.ShapeDtypeStruct((n, d), F32),
        scratch_shapes=[pltpu.VMEM((d, tn), F32),
                        pltpu.VMEM((eb, tn), F32),
                        pltpu.VMEM((eb, tn), BF16),
                        pltpu.VMEM((eb // 2, tn), I32)],
        compiler_params=_params(("arbitrary", "arbitrary")),
        name="peer_dense",
    )(ht, u, vt, rank, by_rows(cnt), by_rows(a), b, x2, gate, fgain)


def _peer(x2, shift, scale, gate, w_q, sub_keys, u_tab, v_tab, fgain, seq, final):
    tn = min(512, seq)
    ht, rank, cnt, a, b = _peer_route(x2, shift, scale, w_q.astype(BF16), sub_keys.astype(BF16), seq,
                                      min(256, seq))
    return _peer_dense(ht, u_tab.astype(BF16), v_tab.T.astype(BF16), rank, cnt, a, b, x2, gate, fgain,
                       seq, tn, P_EXPERT_BLOCK, final)


def kernel(x, c, a_w_in, a_kv_norm, a_w_uk, a_w_uv, a_w_out, b_w_in, b_conv, b_a_log, b_dt_bias,
           b_out_norm, b_w_out, p_w_q, p_sub_keys, p_u, p_v, ada_w, ada_b, final_norm):
    batch, seq, d = x.shape
    n = batch * seq
    depth = ada_w.shape[0]
    x2 = x.reshape(n, d)
    mod = _adaln(c, ada_w, ada_b).reshape(depth, batch, 6, 1, d)
    fgain = final_norm.reshape(1, d)
    tm = min(512, seq)
    for layer in range(depth):
        sh_m, sc_m, g_m, sh_f, sc_f, g_f = (mod[layer, :, k] for k in range(6))
        j = layer // 2
        if layer % 2 == 0:
            w_in = a_w_in[j]
            hr = A_HEADS * A_HEAD_DIM
            c0, c1, c2 = hr + A_KV_RANK, hr + A_KV_RANK + 512, hr + A_KV_RANK + 512 + A_IDX_DIM + A_IDX_HEADS
            w_a = jnp.concatenate([w_in[:, :hr], w_in[:, c0:c1], w_in[:, hr:c0], w_in[:, c1:c2],
                                   jnp.zeros((d, 1920 - c2), w_in.dtype)], axis=1).astype(BF16)
            proj = _norm_mod_matmul(x2, sh_m, sc_m, w_a, BF16, seq, tm, 1920)
            x2 = _dsa(proj, x2, g_m, a_kv_norm[j].reshape(1, -1),
                      a_w_uk[j].transpose(1, 2, 0).astype(BF16), a_w_uv[j].transpose(1, 0, 2).astype(BF16),
                      a_w_out[j].astype(BF16), batch, seq)
        else:
            w_in = b_w_in[j]
            nqkvz = 2 * B_K_HEADS * B_HEAD_DIM + 2 * B_V_HEADS * B_HEAD_DIM
            w_g = jnp.concatenate([w_in[:, nqkvz:], jnp.zeros((d, LANES - 2 * B_V_HEADS), w_in.dtype)], axis=1)
            qkvz = _norm_mod_matmul(x2, sh_m, sc_m, w_in[:, :nqkvz].astype(BF16), BF16, seq, tm, 1024)
            gates = _norm_mod_matmul(x2, sh_m, sc_m, w_g.astype(BF16), F32, seq, tm, LANES)
            gnat, gtr = _gdn_gates(gates, b_a_log[j], b_dt_bias[j], batch, seq)
            onorm = _gdn(qkvz, b_conv[j].reshape(B_CONV, -1), gnat, gtr, b_out_norm[j].reshape(1, -1), batch, seq)
            x2 = _matmul_residual(onorm, b_w_out[j].astype(BF16), x2, g_m, seq, tm)
        x2 = _peer(x2, sh_f, sc_f, g_f, p_w_q[layer], p_sub_keys[layer], p_u[layer], p_v[layer], fgain, seq,
                   final=(layer == depth - 1))
    return x2.reshape(batch, seq, d)
```

```python
import functools

import jax
import jax.numpy as jnp
from jax import lax
from jax.experimental import pallas as pl
from jax.experimental.pallas import tpu as pltpu

F32 = jnp.float32
BF16 = jnp.bfloat16
I32 = jnp.int32
HIGHEST = lax.Precision.HIGHEST

RMS_EPS = 1e-6
CHUNK = 64
LANES = 128
SUBLANES = 8
NEG_BIG = -1e30
SOFTMAX_MIN_SUM = 1e-30

A_HEADS = 16
A_HEAD_DIM = 64
A_KV_RANK = 256
A_IDX_HEADS = 8
A_IDX_DIM = 64
A_TOPK_MAX = 256
A_QBLOCK = 128
A_KTILE = 256

B_K_HEADS = 8
B_V_HEADS = 16
B_HEAD_DIM = 128
B_CONV = 4
B_SUPER = 256
B_GROUPS_IN_FLIGHT = 4

P_HEADS = 8
P_N_KEYS = 128
P_HALF = 128
P_TOPK = 16
P_TOKEN_TILE = 512
P_EXPERT_BLOCK = 1024
P_SUB_ROWS = 4
P_SLAB_GROUP = 4

INT_MIN = int(jnp.iinfo(jnp.int32).min)

_NT = (((1,), (1,)), ((), ()))
_TN = (((0,), (0,)), ((), ()))


def _vmem_limit(mib):
    return pltpu.CompilerParams(vmem_limit_bytes=mib * 1024 * 1024)


def _params(sem, mib=48):
    return pltpu.CompilerParams(dimension_semantics=sem, vmem_limit_bytes=mib * 1024 * 1024)


def _dot(a, b):
    return jnp.dot(a, b, preferred_element_type=F32)


def _dot_nt(a, b):
    return lax.dot_general(a, b, _NT, preferred_element_type=F32)


def _split_bf16(a):
    hi = a.astype(BF16)
    lo = (a - hi.astype(F32)).astype(BF16)
    return hi, lo


def _dot3(a, b):
    ah, al = _split_bf16(a)
    bh, bl = _split_bf16(b)
    return _dot(ah, bh) + (_dot(ah, bl) + _dot(al, bh))


def _rms(x):
    return x * lax.rsqrt(jnp.mean(x * x, axis=-1, keepdims=True) + RMS_EPS)


def _adaln_kernel(c_ref, w_ref, b_ref, o_ref):
    c = c_ref[...]
    cond = c * jax.nn.sigmoid(c)
    o_ref[0] = jnp.dot(cond, w_ref[0], preferred_element_type=F32, precision=HIGHEST) + b_ref[0]


def _adaln(c, ada_w, ada_b):
    depth, d, d6 = ada_w.shape
    b = c.shape[0]
    tn = 1536
    return pl.pallas_call(
        _adaln_kernel,
        grid=(depth, d6 // tn),
        in_specs=[pl.BlockSpec((b, d), lambda l, j: (0, 0)),
                  pl.BlockSpec((1, d, tn), lambda l, j: (l, 0, j)),
                  pl.BlockSpec((1, 1, tn), lambda l, j: (l, 0, j))],
        out_specs=pl.BlockSpec((1, b, tn), lambda l, j: (l, 0, j)),
        out_shape=jax.ShapeDtypeStruct((depth, b, d6), F32),
        compiler_params=_params(("arbitrary", "arbitrary")),
        name="adaln",
    )(c, ada_w, ada_b.reshape(depth, 1, d6))


def _nmm_kernel(x_ref, sh_ref, sc_ref, w_ref, o_ref, h_ref):
    @pl.when(pl.program_id(1) == 0)
    def _():
        h = _rms(x_ref[...]) * (1.0 + sc_ref[0]) + sh_ref[0]
        h_ref[...] = h.astype(BF16)

    o_ref[...] = _dot(h_ref[...], w_ref[...]).astype(o_ref.dtype)


def _norm_mod_matmul(x2, shift, scale, w, out_dtype, seq, tm, tn):
    n, d = x2.shape
    n_out = w.shape[1]
    per_b = seq // tm
    return pl.pallas_call(
        _nmm_kernel,
        grid=(n // tm, n_out // tn),
        in_specs=[pl.BlockSpec((tm, d), lambda i, j: (i, 0)),
                  pl.BlockSpec((1, 1, d), lambda i, j: (i // per_b, 0, 0)),
                  pl.BlockSpec((1, 1, d), lambda i, j: (i // per_b, 0, 0)),
                  pl.BlockSpec((d, tn), lambda i, j: (0, j))],
        out_specs=pl.BlockSpec((tm, tn), lambda i, j: (i, j)),
        out_shape=jax.ShapeDtypeStruct((n, n_out), out_dtype),
        scratch_shapes=[pltpu.VMEM((tm, d), BF16)],
        compiler_params=_params(("arbitrary", "arbitrary")),
        name="norm_mod_matmul",
    )(x2, shift, scale, w)


def _mmres_kernel(a_ref, w_ref, x_ref, g_ref, o_ref):
    o_ref[...] = x_ref[...] + g_ref[0] * _dot(a_ref[...], w_ref[...])


def _matmul_residual(a, w, x2, gate, seq, tm):
    n, k = a.shape
    d = w.shape[1]
    per_b = seq // tm
    return pl.pallas_call(
        _mmres_kernel,
        grid=(n // tm,),
        in_specs=[pl.BlockSpec((tm, k), lambda i: (i, 0)),
                  pl.BlockSpec((k, d), lambda i: (0, 0)),
                  pl.BlockSpec((tm, d), lambda i: (i, 0)),
                  pl.BlockSpec((1, 1, d), lambda i: (i // per_b, 0, 0))],
        out_specs=pl.BlockSpec((tm, d), lambda i: (i, 0)),
        out_shape=jax.ShapeDtypeStruct((n, d), F32),
        compiler_params=_params(("arbitrary",)),
        name="matmul_residual",
    )(a, w, x2, gate)


def _dsa_kernel(q_ref, qidx_ref, kwq_ref, ckv_ref, kw_ref, gain_ref, wuk_ref, wuv_ref, wout_ref,
                x_ref, gm_ref, o_ref,
                ckvn_ref, kn_ref, keys_ref, bound_ref, ql_ref, m_ref, l_ref, acc_ref, *, n_sel):
    qi = pl.program_id(1)
    tq = A_QBLOCK
    n_kt = qi + 1
    seq = keys_ref.shape[0]

    @pl.when(qi == 0)
    def _():
        cn = (_rms(ckv_ref[...].astype(F32)) * gain_ref[...]).astype(BF16)
        ckvn_ref[...] = cn
        cf = cn.astype(F32)
        kn2 = jnp.max(jnp.sum(cf * cf, axis=1, keepdims=True), axis=0, keepdims=True)
        kn_ref[...] = jnp.broadcast_to(kn2, kn_ref.shape)

    krow = lax.broadcasted_iota(I32, (LANES, tq), 0)
    qlane = lax.broadcasted_iota(I32, (LANES, tq), 1)
    qh = [qidx_ref[:, h * A_IDX_DIM:(h + 1) * A_IDX_DIM] for h in range(A_IDX_HEADS)]
    w_t = kwq_ref[...].astype(F32).T * ((A_IDX_HEADS ** -0.5) * (A_IDX_DIM ** -0.5))
    wrow = [w_t[A_IDX_DIM + h:A_IDX_DIM + h + 1, :] for h in range(A_IDX_HEADS)]
    diag_inadm = (krow >= CHUNK) & (qlane < CHUNK)

    n_kt2 = (n_kt + 1) // 2

    def score_pair(t, carry):
        kts = [2 * t, 2 * t + 1]
        offs = [pl.multiple_of(kt * LANES, LANES) for kt in kts]
        kt_k = [kw_ref[pl.ds(off, LANES), 0:A_IDX_DIM] for off in offs]
        sc = [jnp.zeros((LANES, tq), F32) for _ in kts]
        for h in range(A_IDX_HEADS):
            for j in range(2):
                sc[j] = sc[j] + wrow[h] * jnp.maximum(_dot_nt(kt_k[j], qh[h]), 0.0)
        for j in range(2):
            bits = lax.bitcast_convert_type(sc[j], I32)
            key = jnp.where(bits < 0, bits ^ jnp.int32(0x7FFFFFFF), bits)
            inadm = (diag_inadm & (kts[j] == qi)) | (kts[j] > qi)
            keys_ref[pl.ds(offs[j], LANES), :] = jnp.where(inadm, jnp.int32(INT_MIN), key)
        return carry

    lax.fori_loop(0, n_kt2, score_pair, 0)

    kidx = lax.broadcasted_iota(I32, (A_KTILE, tq), 0)

    def count(pred):
        def body(t, acc):
            off = pl.multiple_of(t * A_KTILE, A_KTILE)
            return acc + jnp.where(pred(keys_ref[pl.ds(off, A_KTILE), :], off), 1.0, 0.0)
        acc = lax.fori_loop(0, n_kt2, body, jnp.zeros((A_KTILE, tq), F32))
        return jnp.sum(acc, axis=0, keepdims=True)

    def count_ge(cand):
        cb = jnp.broadcast_to(cand, (A_KTILE, tq))
        return count(lambda k, off: k >= cb)

    kf = float(n_sel)
    zero = jnp.zeros((1, tq), I32)
    ans = jnp.where(count_ge(zero) >= kf, zero, jnp.int32(INT_MIN))

    def bit_body(b, ans):
        cand = ans + jnp.left_shift(jnp.int32(1), 30 - b)
        return jnp.where(count_ge(cand) >= kf, cand, ans)

    ans = lax.fori_loop(0, 31, bit_body, ans)
    thr = jnp.maximum(ans, jnp.int32(INT_MIN + 1))
    thr_k = jnp.broadcast_to(thr, (A_KTILE, tq))
    n_gt = count(lambda k, off: k > thr_k)
    n_ge = count(lambda k, off: k >= thr_k)
    need = kf - n_gt

    bound_ref[...] = jnp.full(bound_ref.shape, float(2 * seq), F32)

    @pl.when(jnp.max(n_ge) > kf)
    def _():
        def eq_before(bound):
            bb = jnp.broadcast_to(bound, (A_KTILE, tq))
            return count(lambda k, off: (k == thr_k) & ((off + kidx).astype(F32) < bb))

        nbits = max(1, (seq - 1).bit_length())

        def jb(b, bound):
            cand = bound + jnp.left_shift(jnp.int32(1), nbits - b).astype(F32)
            return jnp.where(eq_before(cand) <= need, cand, bound)

        bound_ref[...] = jnp.broadcast_to(lax.fori_loop(0, nbits + 1, jb, jnp.zeros((1, tq), F32)), bound_ref.shape)

    bound_b = jnp.broadcast_to(bound_ref[0:1, :], (A_KTILE, tq))

    qn2 = jnp.zeros((tq, 1), F32)
    for h in range(A_HEADS):
        qlat = _dot(q_ref[:, h * A_HEAD_DIM:(h + 1) * A_HEAD_DIM], wuk_ref[h]) * (A_HEAD_DIM ** -0.5)
        qlb = qlat.astype(BF16)
        ql_ref[h * tq:(h + 1) * tq, :] = qlb
        qf = qlb.astype(F32)
        qn2 = jnp.maximum(qn2, jnp.sum(qf * qf, axis=1, keepdims=True))

    def selected(key, off):
        return (key > thr_k) | ((key == thr_k) & ((off + kidx).astype(F32) < bound_b))

    logit_cap = jnp.sqrt(qn2 * kn_ref[:, 0:1]) * (1.0 + 2.0 ** -7) + 2.0 ** -20
    cap_b = jnp.broadcast_to(jnp.broadcast_to(logit_cap, (tq, LANES)).T[0:1, :], (A_KTILE, tq))
    l_ref[...] = jnp.zeros(l_ref.shape, F32)
    acc_ref[...] = jnp.zeros(acc_ref.shape, F32)

    def att_tile_capped(t, carry):
        off = pl.multiple_of(t * A_KTILE, A_KTILE)
        kv = ckvn_ref[pl.ds(off, A_KTILE), :]
        bias = jnp.where(selected(keys_ref[pl.ds(off, A_KTILE), :], off), -cap_b, NEG_BIG).T
        p = jnp.exp(_dot_nt(ql_ref[...], kv).reshape(A_HEADS, tq, A_KTILE) + bias[None])
        l_ref[...] += jnp.sum(p, axis=2, keepdims=True).reshape(A_HEADS * tq, 1)
        acc_ref[...] += _dot(p.reshape(A_HEADS * tq, A_KTILE).astype(BF16), kv)
        return carry

    lax.fori_loop(0, n_kt2, att_tile_capped, 0)

    @pl.when(jnp.logical_not(jnp.min(l_ref[...]) > SOFTMAX_MIN_SUM))
    def _():
        m_ref[...] = jnp.full(m_ref.shape, NEG_BIG, F32)
        l_ref[...] = jnp.zeros(l_ref.shape, F32)
        acc_ref[...] = jnp.zeros(acc_ref.shape, F32)

        def att_tile(t, carry):
            off = pl.multiple_of(t * A_KTILE, A_KTILE)
            kv = ckvn_ref[pl.ds(off, A_KTILE), :]
            bias = jnp.where(selected(keys_ref[pl.ds(off, A_KTILE), :], off), 0.0, NEG_BIG).T
            s = _dot_nt(ql_ref[...], kv).reshape(A_HEADS, tq, A_KTILE) + bias[None]
            m_old = m_ref[...].reshape(A_HEADS, tq, 1)
            m_new = jnp.maximum(m_old, jnp.max(s, axis=2, keepdims=True))
            p = jnp.exp(s - m_new)
            alpha = jnp.exp(m_old - m_new)
            l_ref[...] = (alpha * l_ref[...].reshape(A_HEADS, tq, 1)
                          + jnp.sum(p, axis=2, keepdims=True)).reshape(A_HEADS * tq, 1)
            m_ref[...] = m_new.reshape(A_HEADS * tq, 1)
            pv = _dot(p.reshape(A_HEADS * tq, A_KTILE).astype(BF16), kv)
            acc_ref[...] = alpha.reshape(A_HEADS * tq, 1) * acc_ref[...] + pv
            return carry

        lax.fori_loop(0, n_kt2, att_tile, 0)

    outs = []
    for h in range(A_HEADS):
        rs = slice(h * tq, (h + 1) * tq)
        o_lat = acc_ref[rs, :] / l_ref[rs, :]
        outs.append(_dot(o_lat.astype(BF16), wuv_ref[h]))
    att = jnp.concatenate(outs, axis=1)
    o_ref[...] = x_ref[...] + gm_ref[0] * _dot(att.astype(BF16), wout_ref[...])


def _dsa(proj, x2, gate, kv_gain, wuk_t, wuv, wout, batch, seq):
    n, d = x2.shape
    nq = seq // A_QBLOCK
    hr = A_HEADS * A_HEAD_DIM
    n_sel = min(A_TOPK_MAX, seq // 4)
    return pl.pallas_call(
        functools.partial(_dsa_kernel, n_sel=n_sel),
        grid=(batch, nq),
        in_specs=[pl.BlockSpec((A_QBLOCK, hr), lambda b, i: (b * nq + i, 0)),
                  pl.BlockSpec((A_QBLOCK, 512), lambda b, i: (b * nq + i, 2)),
                  pl.BlockSpec((A_QBLOCK, LANES), lambda b, i: (b * nq + i, 14)),
                  pl.BlockSpec((seq, A_KV_RANK), lambda b, i: (b, 6)),
                  pl.BlockSpec((seq, LANES), lambda b, i: (b, 14)),
                  pl.BlockSpec((1, A_KV_RANK), lambda b, i: (0, 0)),
                  pl.BlockSpec((A_HEADS, A_HEAD_DIM, A_KV_RANK), lambda b, i: (0, 0, 0)),
                  pl.BlockSpec((A_HEADS, A_KV_RANK, A_HEAD_DIM), lambda b, i: (0, 0, 0)),
                  pl.BlockSpec((hr, d), lambda b, i: (0, 0)),
                  pl.BlockSpec((A_QBLOCK, d), lambda b, i: (b * nq + i, 0)),
                  pl.BlockSpec((1, 1, d), lambda b, i: (b, 0, 0))],
        out_specs=pl.BlockSpec((A_QBLOCK, d), lambda b, i: (b * nq + i, 0)),
        out_shape=jax.ShapeDtypeStruct((n, d), F32),
        scratch_shapes=[pltpu.VMEM((seq, A_KV_RANK), BF16),
                        pltpu.VMEM((1, LANES), F32),
                        pltpu.VMEM((seq, A_QBLOCK), I32),
                        pltpu.VMEM((SUBLANES, A_QBLOCK), F32),
                        pltpu.VMEM((A_HEADS * A_QBLOCK, A_KV_RANK), BF16),
                        pltpu.VMEM((A_HEADS * A_QBLOCK, 1), F32),
                        pltpu.VMEM((A_HEADS * A_QBLOCK, 1), F32),
                        pltpu.VMEM((A_HEADS * A_QBLOCK, A_KV_RANK), F32)],
        compiler_params=_params(("arbitrary", "arbitrary")),
        name="dsa_attention",
    )(proj, proj, proj, proj, proj, kv_gain, wuk_t, wuv, wout, x2, gate)


def _gdn_gate_kernel(g_ref, alog_ref, dtb_ref, nat_ref, tr_ref):
    g = g_ref[...]
    col = lax.broadcasted_iota(I32, g.shape, 1)
    beta = jax.nn.sigmoid(g)
    z = g + dtb_ref[...]
    softplus = jnp.maximum(z, 0.0) + jnp.log1p(jnp.exp(-jnp.abs(z)))
    gd = -jnp.exp(alog_ref[...]) * softplus
    r = lax.broadcasted_iota(I32, (B_SUPER, B_SUPER), 0)
    c = lax.broadcasted_iota(I32, (B_SUPER, B_SUPER), 1)
    same = (r >> 6) == (c >> 6)
    tril = jnp.where(same & (c <= r), 1.0, 0.0).astype(F32)
    gc = jnp.dot(tril, gd, preferred_element_type=F32, precision=HIGHEST)
    nat = jnp.where(col < B_V_HEADS, beta, gc)
    nat_ref[...] = nat
    tr_ref[0, 0] = nat.T


def _gdn_gates(gates, a_log, dt_bias, batch, seq):
    n = gates.shape[0]
    ns = seq // B_SUPER
    pad = lambda v: jnp.zeros((1, LANES), F32).at[0, B_V_HEADS:2 * B_V_HEADS].set(v.astype(F32))
    return pl.pallas_call(
        _gdn_gate_kernel,
        grid=(batch, ns),
        in_specs=[pl.BlockSpec((B_SUPER, LANES), lambda b, s: (b * ns + s, 0)),
                  pl.BlockSpec((1, LANES), lambda b, s: (0, 0)),
                  pl.BlockSpec((1, LANES), lambda b, s: (0, 0))],
        out_specs=[pl.BlockSpec((B_SUPER, LANES), lambda b, s: (b * ns + s, 0)),
                   pl.BlockSpec((1, 1, LANES, B_SUPER), lambda b, s: (b, s, 0, 0))],
        out_shape=[jax.ShapeDtypeStruct((n, LANES), F32),
                   jax.ShapeDtypeStruct((batch, ns, LANES, B_SUPER), F32)],
        compiler_params=_params(("arbitrary", "arbitrary")),
        name="gdn_gates",
    )(gates, pad(a_log), pad(dt_bias))


def _conv_silu(raw_ref, w_ref, pad_ref):
    t = raw_ref.shape[0]
    pad_ref[0:8, :] = jnp.zeros((8, LANES), F32)
    pad_ref[8:8 + t, :] = raw_ref[...].astype(F32)
    w = w_ref[...]
    y = w[3:4] * pad_ref[8:8 + t, :]
    for j in range(B_CONV - 1):
        y = y + w[j:j + 1] * pad_ref[5 + j:5 + j + t, :]
    return y * jax.nn.sigmoid(y)


def _l2n(y):
    return y * lax.rsqrt(jnp.sum(y * y, axis=-1, keepdims=True) + RMS_EPS)


def _gdn_kernel(q_ref, k_ref, v0_ref, v1_ref, z_ref, wq_ref, wk_ref, wv0_ref, wv1_ref,
                gnat_ref, gtr_ref, gain_ref, o_ref,
                pad_ref, qn_ref, kn_ref, vs_ref, u_ref, wqg_ref, kg_ref, at_ref, el_ref, oo_ref):
    hk = pl.program_id(1)
    seq = q_ref.shape[0]
    ns = seq // B_SUPER
    nc = seq // CHUNK

    qn_ref[...] = _l2n(_conv_silu(q_ref, wq_ref, pad_ref)) * (B_HEAD_DIM ** -0.5)
    kn_ref[...] = _l2n(_conv_silu(k_ref, wk_ref, pad_ref))
    vs_ref[0] = _conv_silu(v0_ref, wv0_ref, pad_ref)
    vs_ref[1] = _conv_silu(v1_ref, wv1_ref, pad_ref)

    r = lax.broadcasted_iota(I32, (B_SUPER, B_SUPER), 0)
    c = lax.broadcasted_iota(I32, (B_SUPER, B_SUPER), 1)
    same = (r >> 6) == (c >> 6)
    tril = same & (c <= r)
    strict = same & (c < r)
    is_last = c == (r | (CHUNK - 1))
    eye = jnp.where(r == c, 1.0, 0.0).astype(F32)
    lane = lax.broadcasted_iota(I32, (B_SUPER, LANES), 1)

    in_flight = min(B_GROUPS_IN_FLIGHT, ns)

    def precompute(s2, carry):
        chains = []
        for j in range(in_flight):
            s = s2 * in_flight + j
            rows = pl.ds(pl.multiple_of(s * B_SUPER, B_SUPER), B_SUPER)
            kn = kn_ref[rows, :]
            qn = qn_ref[rows, :]
            knb = kn.astype(BF16)
            kk = _dot_nt(knb, knb)
            qk = _dot_nt(qn.astype(BF16), knb)
            gnat = gnat_ref[rows, :]
            for e in range(2):
                hv = 2 * hk + e
                bcol = jnp.sum(jnp.where(lane == hv, gnat, 0.0), axis=1, keepdims=True)
                gcol = jnp.sum(jnp.where(lane == hv + B_V_HEADS, gnat, 0.0), axis=1, keepdims=True)
                grow = gtr_ref[0, s, pl.ds(hv + B_V_HEADS, 1), :]
                grow_b = jnp.broadcast_to(grow, (B_SUPER, B_SUPER))
                dec = jnp.where(tril, jnp.exp(gcol - grow_b), 0.0)
                a = jnp.where(strict, bcol * kk * dec, 0.0)
                eg = jnp.exp(gcol)
                glast = jnp.sum(jnp.where(is_last, grow_b, 0.0), axis=1, keepdims=True)
                qg = (qn * eg).astype(BF16)
                kg_ref[e, rows, :] = (kn * jnp.exp(glast - gcol)).astype(BF16)
                el_ref[e, rows, :] = jnp.exp(glast)
                attn = jnp.where(tril, qk * dec, 0.0).astype(BF16)
                for c in range(B_SUPER // CHUNK):
                    crows = slice(c * CHUNK, (c + 1) * CHUNK)
                    at_ref[e, pl.ds(pl.multiple_of(s * B_SUPER + c * CHUNK, CHUNK), CHUNK), :] = attn[crows, crows]
                    wqg_ref[e, pl.ds(pl.multiple_of((2 * s) * B_SUPER + (2 * c + 1) * CHUNK, CHUNK), CHUNK), :] = qg[crows]
                chains.append(dict(e=e, s=s, rows=rows, a=a, vb=vs_ref[e, rows, :] * bcol, kb=kn * (bcol * eg)))
        for ch in chains:
            ch["pw"] = -ch["a"]
            ch["tm"] = eye + ch["pw"]
        for _ in range(5):
            for ch in chains:
                pwb = ch["pw"].astype(BF16)
                ch["pw"] = _dot(pwb, pwb)
            for ch in chains:
                ch["tm"] = ch["tm"] + _dot(ch["tm"].astype(BF16), ch["pw"].astype(BF16))
        for ch in chains:
            ch["resid"] = eye - _dot3(eye + ch["a"], ch["tm"])
        for ch in chains:
            ch["tm"] = ch["tm"] + _dot(ch["tm"].astype(BF16), ch["resid"].astype(BF16))
        for ch in chains:
            u_ref[ch["e"], ch["rows"], :] = _dot3(ch["tm"], ch["vb"])
            w = _dot3(ch["tm"], ch["kb"]).astype(BF16)
            for c in range(B_SUPER // CHUNK):
                wrows = pl.ds(pl.multiple_of((2 * ch["s"]) * B_SUPER + (2 * c) * CHUNK, CHUNK), CHUNK)
                wqg_ref[ch["e"], wrows, :] = w[c * CHUNK:(c + 1) * CHUNK]
        return carry

    lax.fori_loop(0, ns // in_flight, precompute, 0)

    def step(n, states):
        rows = pl.ds(pl.multiple_of(n * CHUNK, CHUNK), CHUNK)
        rows2 = pl.ds(pl.multiple_of(n * 2 * CHUNK, 2 * CHUNK), 2 * CHUNK)
        heads = range(2)
        sb = [states[e].astype(BF16) for e in heads]
        ws = [_dot(wqg_ref[e, rows2, :], sb[e]) for e in heads]
        vb = [(u_ref[e, rows, :] - ws[e][:CHUNK]).astype(BF16) for e in heads]
        for e in heads:
            oo_ref[e, rows, :] = ws[e][CHUNK:] + _dot(at_ref[e, rows, :], vb[e])
        return tuple(states[e] * el_ref[e, pl.ds(n * CHUNK, 1), :]
                     + lax.dot_general(kg_ref[e, rows, :], vb[e], _TN, preferred_element_type=F32)
                     for e in heads)

    zero = jnp.zeros((B_HEAD_DIM, B_HEAD_DIM), F32)
    lax.fori_loop(0, nc, step, (zero, zero))

    for e in range(2):
        z = z_ref[:, e * B_HEAD_DIM:(e + 1) * B_HEAD_DIM].astype(F32)
        y = _rms(oo_ref[e]) * gain_ref[...] * (z * jax.nn.sigmoid(z))
        o_ref[:, e * B_HEAD_DIM:(e + 1) * B_HEAD_DIM] = y.astype(o_ref.dtype)


def _gdn(qkvz, conv_w, gnat, gtr, out_gain, batch, seq):
    n = qkvz.shape[0]
    hd = B_HEAD_DIM
    kh = B_K_HEADS
    ns = seq // B_SUPER
    col = lambda f: (lambda b, h: (b, f(h)))
    wcol = lambda f: (lambda b, h: (0, f(h)))
    return pl.pallas_call(
        _gdn_kernel,
        grid=(batch, kh),
        in_specs=[pl.BlockSpec((seq, hd), col(lambda h: h)),
                  pl.BlockSpec((seq, hd), col(lambda h: kh + h)),
                  pl.BlockSpec((seq, hd), col(lambda h: 2 * kh + 2 * h)),
                  pl.BlockSpec((seq, hd), col(lambda h: 2 * kh + 2 * h + 1)),
                  pl.BlockSpec((seq, 2 * hd), col(lambda h: 2 * kh + h)),
                  pl.BlockSpec((B_CONV, hd), wcol(lambda h: h)),
                  pl.BlockSpec((B_CONV, hd), wcol(lambda h: kh + h)),
                  pl.BlockSpec((B_CONV, hd), wcol(lambda h: 2 * kh + 2 * h)),
                  pl.BlockSpec((B_CONV, hd), wcol(lambda h: 2 * kh + 2 * h + 1)),
                  pl.BlockSpec((seq, LANES), lambda b, h: (b, 0)),
                  pl.BlockSpec((1, ns, LANES, B_SUPER), lambda b, h: (b, 0, 0, 0)),
                  pl.BlockSpec((1, hd), lambda b, h: (0, 0))],
        out_specs=pl.BlockSpec((seq, 2 * hd), lambda b, h: (b, h)),
        out_shape=jax.ShapeDtypeStruct((n, B_V_HEADS * hd), BF16),
        scratch_shapes=[pltpu.VMEM((seq + 8, hd), F32),
                        pltpu.VMEM((seq, hd), F32),
                        pltpu.VMEM((seq, hd), F32),
                        pltpu.VMEM((2, seq, hd), F32),
                        pltpu.VMEM((2, seq, hd), F32),
                        pltpu.VMEM((2, 2 * seq, hd), BF16),
                        pltpu.VMEM((2, seq, hd), BF16),
                        pltpu.VMEM((2, seq, CHUNK), BF16),
                        pltpu.VMEM((2, seq, 1), F32),
                        pltpu.VMEM((2, seq, hd), F32)],
        compiler_params=_params(("arbitrary", "arbitrary")),
        name="gated_deltanet",
    )(qkvz, qkvz, qkvz, qkvz, qkvz, conv_w, conv_w, conv_w, conv_w, gnat, gtr, out_gain)


_CAND_W = [P_TOPK // (r + 1) for r in range(P_TOPK)]
_CAND_N = sum(_CAND_W)
_CAND_PAD = -(-_CAND_N // 8) * 8


def _top16_ranks(s, iota_k=None):
    rank = jnp.full(s.shape, float(P_TOPK), F32)
    vals = []
    for r in range(P_TOPK):
        m = jnp.max(s, axis=0, keepdims=True)
        hit = s == m
        if iota_k is not None:
            hit = iota_k == jnp.min(jnp.where(hit, iota_k, P_N_KEYS), axis=0, keepdims=True)
        rank = jnp.where(hit, float(r), rank)
        s = jnp.where(hit, -jnp.inf, s)
        vals.append(m)
    return vals, rank


def _route_head(s1, s2, iota_k, iota_k2):
    tn = s1.shape[1]
    v1, rank1 = _top16_ranks(s1, iota_k)
    v2, rank2 = _top16_ranks(s2, iota_k2)
    v2s = jnp.concatenate(v2, axis=0)
    ea = [jnp.exp(v1[r] - v1[0]) for r in range(P_TOPK)]
    eb = jnp.exp(v2s - v2[0])
    neg_pad = jnp.full((_CAND_PAD - _CAND_N, tn), -jnp.inf, F32)
    iota_c = lax.broadcasted_iota(I32, (_CAND_PAD, tn), 0)
    cand = jnp.concatenate([v1[r] + v2s[0:_CAND_W[r]] for r in range(P_TOPK)] + [neg_pad], axis=0)
    gate = jnp.concatenate([ea[r] * eb[0:_CAND_W[r]] for r in range(P_TOPK)] + [jnp.zeros_like(neg_pad)], axis=0)
    sel = jnp.zeros((_CAND_PAD, tn), F32)
    work = cand
    for _ in range(P_TOPK):
        m = jnp.max(work, axis=0, keepdims=True)
        idx = jnp.min(jnp.where(work == m, iota_c, _CAND_PAD), axis=0, keepdims=True)
        hit = iota_c == idx
        sel = jnp.where(hit, 1.0, sel)
        work = jnp.where(hit, -jnp.inf, work)
    inv_z = 1.0 / jnp.sum(sel * gate, axis=0, keepdims=True)
    cnt_t = jnp.zeros(s1.shape, F32)
    off = 0
    for r in range(P_TOPK):
        n_r = jnp.sum(sel[off:off + _CAND_W[r]], axis=0, keepdims=True)
        off += _CAND_W[r]
        cnt_t = jnp.where(rank1 == float(r), n_r, cnt_t)
    a_t = jnp.where(rank1 < float(P_TOPK), jnp.exp(s1 - v1[0]), 0.0)
    b_t = jnp.where(rank2 < float(P_TOPK), jnp.exp(s2 - v2[0]) * (0.5 * inv_z), 0.0)
    n_ranked = jnp.sum(jnp.where(rank1 < float(P_TOPK), 1.0, 0.0) + jnp.where(rank2 < float(P_TOPK), 1.0, 0.0),
                       axis=0, keepdims=True)
    return rank2, cnt_t, a_t, b_t, n_ranked


def _bf16_pair_words(lo, hi):
    bits = lambda v: lax.bitcast_convert_type(v.astype(BF16).astype(F32), I32)
    return bits(hi) | lax.shift_right_logical(bits(lo), 16)


def _peer_route_kernel(x_ref, sh_ref, sc_ref, wq_ref, sk_ref, ht_ref, rank_ref, cnt_ref, a_ref, b_ref):
    tn = x_ref.shape[0]
    h = _rms(x_ref[...]) * (1.0 + sc_ref[0]) + sh_ref[0]
    ht_ref[...] = h.T.astype(BF16)
    qb = _dot(h.astype(BF16), wq_ref[...]).astype(BF16)
    iota_k = lax.broadcasted_iota(I32, (P_N_KEYS, LANES), 0)
    half = P_N_KEYS // 2
    iota_k2 = jnp.where(iota_k < half, 2 * iota_k, 2 * (iota_k - half) + 1)
    for hd in range(P_HEADS):
        base = hd * 2 * P_HALF
        s1 = _dot_nt(sk_ref[hd, 0], qb[:, base:base + P_HALF])
        s2 = _dot_nt(sk_ref[hd, 1], qb[:, base + P_HALF:base + 2 * P_HALF])

        def emit(index_ties):
            most_ranked = jnp.zeros((), F32)
            for t in range(tn // LANES):
                lanes = slice(t * LANES, (t + 1) * LANES)
                rank2, cnt_t, a_t, b_t, n_ranked = _route_head(s1[:, lanes], s2[:, lanes],
                                                               iota_k if index_ties else None,
                                                               iota_k2 if index_ties else None)
                rank_ref[hd, :, lanes] = _bf16_pair_words(rank2[:half], rank2[half:])
                cnt_ref[hd, :, lanes] = _bf16_pair_words(cnt_t, cnt_t)
                a_ref[hd, :, lanes] = _bf16_pair_words(a_t, a_t)
                b_ref[hd, :, lanes] = _bf16_pair_words(b_t[:half], b_t[half:])
                most_ranked = jnp.maximum(most_ranked, jnp.max(n_ranked))
            return most_ranked

        most_ranked = emit(False)

        @pl.when(most_ranked > float(2 * P_TOPK))
        def _():
            emit(True)


def _peer_route(x2, shift, scale, wq, sub_keys, seq, tn):
    n, d = x2.shape
    per_b = seq // tn
    tok = lambda i: (0, 0, i)
    arr = jax.ShapeDtypeStruct((P_HEADS, P_N_KEYS, n), I32)
    arr_pair = jax.ShapeDtypeStruct((P_HEADS, P_N_KEYS // 2, n), I32)
    return pl.pallas_call(
        _peer_route_kernel,
        grid=(n // tn,),
        in_specs=[pl.BlockSpec((tn, d), lambda i: (i, 0)),
                  pl.BlockSpec((1, 1, d), lambda i: (i // per_b, 0, 0)),
                  pl.BlockSpec((1, 1, d), lambda i: (i // per_b, 0, 0)),
                  pl.BlockSpec(wq.shape, lambda i: (0, 0)),
                  pl.BlockSpec(sub_keys.shape, lambda i: (0, 0, 0, 0))],
        out_specs=[pl.BlockSpec((d, tn), lambda i: (0, i)),
                   pl.BlockSpec((P_HEADS, P_N_KEYS // 2, tn), tok),
                   pl.BlockSpec((P_HEADS, P_N_KEYS, tn), tok),
                   pl.BlockSpec((P_HEADS, P_N_KEYS, tn), tok),
                   pl.BlockSpec((P_HEADS, P_N_KEYS // 2, tn), tok)],
        out_shape=[jax.ShapeDtypeStruct((d, n), BF16), arr_pair, arr, arr, arr_pair],
        compiler_params=_params(("arbitrary",)),
        name="peer_route",
    )(x2, shift, scale, wq, sub_keys)


_PAIR_ORDER = tuple(range(0, P_N_KEYS, 2)) + tuple(range(1, P_N_KEYS, 2))


def _peer_dense_kernel(ht_ref, u_ref, vt_ref, rank_ref, cnt_ref, a_ref, b_ref, x_ref, gf_ref, fn_ref,
                       o_ref, acc_ref, act_ref, p_ref, g_ref, *, final):
    e = pl.program_id(1)
    eb = u_ref.shape[0]
    n_i = eb // P_N_KEYS

    @pl.when(e == 0)
    def _():
        acc_ref[...] = jnp.zeros(acc_ref.shape, F32)

    tn = ht_ref.shape[1]
    half = P_N_KEYS // 2
    n_slab = half // SUBLANES
    zero = jnp.zeros((2 * SUBLANES, LANES), BF16)

    def words(ref, hd, rows, lanes):
        return pltpu.bitcast(ref[hd, rows, lanes], BF16)

    def row_tile(ref, hd, ii, lanes):
        return pltpu.bitcast(jnp.broadcast_to(ref[hd, 0, ii:ii + 1, lanes], (SUBLANES, LANES)), BF16)

    def gate_build(i_rows):
        for t in range(tn // LANES):
            lanes = slice(t * LANES, (t + 1) * LANES)
            for s0 in range(0, n_slab, P_SLAB_GROUP):
                slabs = range(s0, s0 + P_SLAB_GROUP)
                g = {(ii, s): zero for ii in i_rows for s in slabs}
                for hd in range(P_HEADS):
                    cnt = {ii: row_tile(cnt_ref, hd, ii, lanes) for ii in i_rows}
                    a = {ii: row_tile(a_ref, hd, ii, lanes) for ii in i_rows}
                    for s in slabs:
                        r = words(rank_ref, hd, slice(s * SUBLANES, (s + 1) * SUBLANES), lanes)
                        b = words(b_ref, hd, slice(s * SUBLANES, (s + 1) * SUBLANES), lanes)
                        for ii in i_rows:
                            g[ii, s] = g[ii, s] + a[ii] * jnp.where(r < cnt[ii], b, zero)
                for ii in i_rows:
                    for s in slabs:
                        wrows = slice(ii * half + s * SUBLANES, ii * half + (s + 1) * SUBLANES)
                        g_ref[wrows, lanes] = pltpu.bitcast(g[ii, s], I32)

    def apply_act(i_rows):
        for ii in i_rows:
            g = pltpu.bitcast(g_ref[ii * half:(ii + 1) * half, :], BF16)
            rows = slice(ii * P_N_KEYS, (ii + 1) * P_N_KEYS)
            x = act_ref[rows, :].astype(BF16)
            p_ref[rows, :] = g * (x * (1.0 + lax.erf(x * (2.0 ** -0.5))))

    subs = [range(i0, i0 + P_SUB_ROWS) for i0 in range(0, n_i, P_SUB_ROWS)]
    rows_of = lambda sub: slice(sub[0] * P_N_KEYS, (sub[-1] + 1) * P_N_KEYS)
    gate_build(subs[0])
    for sub in subs:
        act_ref[rows_of(sub), :] = _dot(u_ref[rows_of(sub), :], ht_ref[...])
    for k, sub in enumerate(subs):
        apply_act(sub)
        if k + 1 < len(subs):
            gate_build(subs[k + 1])
        acc_ref[...] += _dot(vt_ref[:, rows_of(sub)], p_ref[rows_of(sub), :])

    @pl.when(e == pl.num_programs(1) - 1)
    def _():
        xn = x_ref[...] + gf_ref[0] * acc_ref[...].T
        if final:
            xn = _rms(xn) * fn_ref[...]
        o_ref[...] = xn


def _peer_dense(ht, u, vt, rank, cnt, a, b, x2, gate, fgain, seq, tn, eb, final):
    n, d = x2.shape
    n_e = u.shape[0]
    per_b = seq // tn
    tok = lambda t, e: (0, 0, t)
    n_i = eb // P_N_KEYS
    rows_of = lambda t, e: (0, e, 0, t)
    by_rows = lambda v: v.reshape(P_HEADS, P_N_KEYS // n_i, n_i, n)
    return pl.pallas_call(
        functools.partial(_peer_dense_kernel, final=final),
        grid=(n // tn, n_e // eb),
        in_specs=[pl.BlockSpec((d, tn), lambda t, e: (0, t)),
                  pl.BlockSpec((eb, d), lambda t, e: (e, 0)),
                  pl.BlockSpec((d, eb), lambda t, e: (0, e)),
                  pl.BlockSpec((P_HEADS, P_N_KEYS // 2, tn), tok),
                  pl.BlockSpec((P_HEADS, 1, n_i, tn), rows_of),
                  pl.BlockSpec((P_HEADS, 1, n_i, tn), rows_of),
                  pl.BlockSpec((P_HEADS, P_N_KEYS // 2, tn), tok),
                  pl.BlockSpec((tn, d), lambda t, e: (t, 0)),
                  pl.BlockSpec((1, 1, d), lambda t, e: (t // per_b, 0, 0)),
                  pl.BlockSpec((1, d), lambda t, e: (0, 0))],
        out_specs=pl.BlockSpec((tn, d), lambda t, e: (t, 0)),
        out_shape=jax.ShapeDtypeStruct((n, d), F32),
        scratch_shapes=[pltpu.VMEM((d, tn), F32),
                        pltpu.VMEM((eb, tn), F32),
                        pltpu.VMEM((eb, tn), BF16),
                        pltpu.VMEM((eb // 2, tn), I32)],
        compiler_params=_params(("arbitrary", "arbitrary")),
        name="peer_dense",
    )(ht, u, vt, rank, by_rows(cnt), by_rows(a), b, x2, gate, fgain)


def _peer(x2, shift, scale, gate, w_q, sub_keys, u_tab, v_tab, fgain, seq, final):
    tn = min(512, seq)
    sub_keys = sub_keys.at[:, 1].set(sub_keys[:, 1][:, _PAIR_ORDER])
    ht, rank, cnt, a, b = _peer_route(x2, shift, scale, w_q.astype(BF16), sub_keys.astype(BF16), seq,
                                      min(256, seq))
    return _peer_dense(ht, u_tab.astype(BF16), v_tab.T.astype(BF16), rank, cnt, a, b, x2, gate, fgain,
                       seq, tn, P_EXPERT_BLOCK, final)


def kernel(x, c, a_w_in, a_kv_norm, a_w_uk, a_w_uv, a_w_out, b_w_in, b_conv, b_a_log, b_dt_bias,
           b_out_norm, b_w_out, p_w_q, p_sub_keys, p_u, p_v, ada_w, ada_b, final_norm):
    batch, seq, d = x.shape
    n = batch * seq
    depth = ada_w.shape[0]
    x2 = x.reshape(n, d)
    mod = _adaln(c, ada_w, ada_b).reshape(depth, batch, 6, 1, d)
    fgain = final_norm.reshape(1, d)
    tm = min(512, seq)
    for layer in range(depth):
        sh_m, sc_m, g_m, sh_f, sc_f, g_f = (mod[layer, :, k] for k in range(6))
        j = layer // 2
        if layer % 2 == 0:
            w_in = a_w_in[j]
            hr = A_HEADS * A_HEAD_DIM
            c0, c1, c2 = hr + A_KV_RANK, hr + A_KV_RANK + 512, hr + A_KV_RANK + 512 + A_IDX_DIM + A_IDX_HEADS
            w_a = jnp.concatenate([w_in[:, :hr], w_in[:, c0:c1], w_in[:, hr:c0], w_in[:, c1:c2],
                                   jnp.zeros((d, 1920 - c2), w_in.dtype)], axis=1).astype(BF16)
            proj = _norm_mod_matmul(x2, sh_m, sc_m, w_a, BF16, seq, tm, 1920)
            x2 = _dsa(proj, x2, g_m, a_kv_norm[j].reshape(1, -1),
                      a_w_uk[j].transpose(1, 2, 0).astype(BF16), a_w_uv[j].transpose(1, 0, 2).astype(BF16),
                      a_w_out[j].astype(BF16), batch, seq)
        else:
            w_in = b_w_in[j]
            nqkvz = 2 * B_K_HEADS * B_HEAD_DIM + 2 * B_V_HEADS * B_HEAD_DIM
            w_g = jnp.concatenate([w_in[:, nqkvz:], jnp.zeros((d, LANES - 2 * B_V_HEADS), w_in.dtype)], axis=1)
            qkvz = _norm_mod_matmul(x2, sh_m, sc_m, w_in[:, :nqkvz].astype(BF16), BF16, seq, tm, 2048)
            gates = _norm_mod_matmul(x2, sh_m, sc_m, w_g.astype(BF16), F32, seq, tm, LANES)
            gnat, gtr = _gdn_gates(gates, b_a_log[j], b_dt_bias[j], batch, seq)
            onorm = _gdn(qkvz, b_conv[j].reshape(B_CONV, -1), gnat, gtr, b_out_norm[j].reshape(1, -1), batch, seq)
            x2 = _matmul_residual(onorm, b_w_out[j].astype(BF16), x2, g_m, seq, tm)
        x2 = _peer(x2, sh_f, sc_f, g_f, p_w_q[layer], p_sub_keys[layer], p_u[layer], p_v[layer], fgain, seq,
                   final=(layer == depth - 1))
    return x2.reshape(batch, seq, d)
```

```python
import functools

import jax
import jax.numpy as jnp
from jax import lax
from jax.experimental import pallas as pl
from jax.experimental.pallas import tpu as pltpu

F32 = jnp.float32
BF16 = jnp.bfloat16
I32 = jnp.int32
HIGHEST = lax.Precision.HIGHEST

RMS_EPS = 1e-6
CHUNK = 64
LANES = 128
SUBLANES = 8
NEG_BIG = -1e30
SOFTMAX_MIN_SUM = 1e-30

A_HEADS = 16
A_HEAD_DIM = 64
A_KV_RANK = 256
A_IDX_HEADS = 8
A_IDX_DIM = 64
A_TOPK_MAX = 256
A_QBLOCK = 128
A_KTILE = 256

B_K_HEADS = 8
B_V_HEADS = 16
B_HEAD_DIM = 128
B_CONV = 4
B_SUPER = 256
B_GROUPS_IN_FLIGHT = 4

P_HEADS = 8
P_N_KEYS = 128
P_HALF = 128
P_TOPK = 16
P_TOKEN_TILE = 512
P_EXPERT_BLOCK = 1024
P_SUB_ROWS = 4
P_SLAB_GROUP = 4

INT_MIN = int(jnp.iinfo(jnp.int32).min)

_NT = (((1,), (1,)), ((), ()))
_TN = (((0,), (0,)), ((), ()))


def _vmem_limit(mib):
    return pltpu.CompilerParams(vmem_limit_bytes=mib * 1024 * 1024)


def _params(sem, mib=48):
    return pltpu.CompilerParams(dimension_semantics=sem, vmem_limit_bytes=mib * 1024 * 1024)


def _dot(a, b):
    return jnp.dot(a, b, preferred_element_type=F32)


def _dot_nt(a, b):
    return lax.dot_general(a, b, _NT, preferred_element_type=F32)


def _split_bf16(a):
    hi = a.astype(BF16)
    lo = (a - hi.astype(F32)).astype(BF16)
    return hi, lo


def _dot3(a, b):
    ah, al = _split_bf16(a)
    bh, bl = _split_bf16(b)
    return _dot(ah, bh) + (_dot(ah, bl) + _dot(al, bh))


def _rms(x):
    return x * lax.rsqrt(jnp.mean(x * x, axis=-1, keepdims=True) + RMS_EPS)


def _adaln_kernel(c_ref, w_ref, b_ref, o_ref):
    c = c_ref[...]
    cond = c * jax.nn.sigmoid(c)
    o_ref[0] = jnp.dot(cond, w_ref[0], preferred_element_type=F32, precision=HIGHEST) + b_ref[0]


def _adaln(c, ada_w, ada_b):
    depth, d, d6 = ada_w.shape
    b = c.shape[0]
    tn = 1536
    return pl.pallas_call(
        _adaln_kernel,
        grid=(depth, d6 // tn),
        in_specs=[pl.BlockSpec((b, d), lambda l, j: (0, 0)),
                  pl.BlockSpec((1, d, tn), lambda l, j: (l, 0, j)),
                  pl.BlockSpec((1, 1, tn), lambda l, j: (l, 0, j))],
        out_specs=pl.BlockSpec((1, b, tn), lambda l, j: (l, 0, j)),
        out_shape=jax.ShapeDtypeStruct((depth, b, d6), F32),
        compiler_params=_params(("arbitrary", "arbitrary")),
        name="adaln",
    )(c, ada_w, ada_b.reshape(depth, 1, d6))


def _nmm_kernel(x_ref, sh_ref, sc_ref, w_ref, o_ref, h_ref):
    @pl.when(pl.program_id(1) == 0)
    def _():
        h = _rms(x_ref[...]) * (1.0 + sc_ref[0]) + sh_ref[0]
        h_ref[...] = h.astype(BF16)

    o_ref[...] = _dot(h_ref[...], w_ref[...]).astype(o_ref.dtype)


def _norm_mod_matmul(x2, shift, scale, w, out_dtype, seq, tm, tn):
    n, d = x2.shape
    n_out = w.shape[1]
    per_b = seq // tm
    return pl.pallas_call(
        _nmm_kernel,
        grid=(n // tm, n_out // tn),
        in_specs=[pl.BlockSpec((tm, d), lambda i, j: (i, 0)),
                  pl.BlockSpec((1, 1, d), lambda i, j: (i // per_b, 0, 0)),
                  pl.BlockSpec((1, 1, d), lambda i, j: (i // per_b, 0, 0)),
                  pl.BlockSpec((d, tn), lambda i, j: (0, j))],
        out_specs=pl.BlockSpec((tm, tn), lambda i, j: (i, j)),
        out_shape=jax.ShapeDtypeStruct((n, n_out), out_dtype),
        scratch_shapes=[pltpu.VMEM((tm, d), BF16)],
        compiler_params=_params(("arbitrary", "arbitrary")),
        name="norm_mod_matmul",
    )(x2, shift, scale, w)


def _mmres_kernel(a_ref, w_ref, x_ref, g_ref, o_ref):
    o_ref[...] = x_ref[...] + g_ref[0] * _dot(a_ref[...], w_ref[...])


def _matmul_residual(a, w, x2, gate, seq, tm):
    n, k = a.shape
    d = w.shape[1]
    per_b = seq // tm
    return pl.pallas_call(
        _mmres_kernel,
        grid=(n // tm,),
        in_specs=[pl.BlockSpec((tm, k), lambda i: (i, 0)),
                  pl.BlockSpec((k, d), lambda i: (0, 0)),
                  pl.BlockSpec((tm, d), lambda i: (i, 0)),
                  pl.BlockSpec((1, 1, d), lambda i: (i // per_b, 0, 0))],
        out_specs=pl.BlockSpec((tm, d), lambda i: (i, 0)),
        out_shape=jax.ShapeDtypeStruct((n, d), F32),
        compiler_params=_params(("arbitrary",)),
        name="matmul_residual",
    )(a, w, x2, gate)


def _dsa_kernel(q_ref, qidx_ref, kwq_ref, ckv_ref, kw_ref, gain_ref, wuk_ref, wuv_ref, wout_ref,
                x_ref, gm_ref, o_ref,
                ckvn_ref, kn_ref, keys_ref, bound_ref, ql_ref, m_ref, l_ref, acc_ref, *, n_sel):
    qi = pl.program_id(1)
    tq = A_QBLOCK
    n_kt = qi + 1
    seq = keys_ref.shape[0]

    @pl.when(qi == 0)
    def _():
        cn = (_rms(ckv_ref[...].astype(F32)) * gain_ref[...]).astype(BF16)
        ckvn_ref[...] = cn
        cf = cn.astype(F32)
        kn2 = jnp.max(jnp.sum(cf * cf, axis=1, keepdims=True), axis=0, keepdims=True)
        kn_ref[...] = jnp.broadcast_to(kn2, kn_ref.shape)

    krow = lax.broadcasted_iota(I32, (LANES, tq), 0)
    qlane = lax.broadcasted_iota(I32, (LANES, tq), 1)
    qh = [qidx_ref[:, h * A_IDX_DIM:(h + 1) * A_IDX_DIM] for h in range(A_IDX_HEADS)]
    w_t = kwq_ref[...].astype(F32).T * ((A_IDX_HEADS ** -0.5) * (A_IDX_DIM ** -0.5))
    wrow = [w_t[A_IDX_DIM + h:A_IDX_DIM + h + 1, :] for h in range(A_IDX_HEADS)]
    diag_inadm = (krow >= CHUNK) & (qlane < CHUNK)

    n_kt2 = (n_kt + 1) // 2

    def score_pair(t, carry):
        kts = [2 * t, 2 * t + 1]
        offs = [pl.multiple_of(kt * LANES, LANES) for kt in kts]
        kt_k = [kw_ref[pl.ds(off, LANES), 0:A_IDX_DIM] for off in offs]
        sc = [jnp.zeros((LANES, tq), F32) for _ in kts]
        for h in range(A_IDX_HEADS):
            for j in range(2):
                sc[j] = sc[j] + wrow[h] * jnp.maximum(_dot_nt(kt_k[j], qh[h]), 0.0)
        for j in range(2):
            bits = lax.bitcast_convert_type(sc[j], I32)
            key = jnp.where(bits < 0, bits ^ jnp.int32(0x7FFFFFFF), bits)
            inadm = (diag_inadm & (kts[j] == qi)) | (kts[j] > qi)
            keys_ref[pl.ds(offs[j], LANES), :] = jnp.where(inadm, jnp.int32(INT_MIN), key)
        return carry

    lax.fori_loop(0, n_kt2, score_pair, 0)

    kidx = lax.broadcasted_iota(I32, (A_KTILE, tq), 0)

    def count(pred):
        def body(t, acc):
            off = pl.multiple_of(t * A_KTILE, A_KTILE)
            return acc + jnp.where(pred(keys_ref[pl.ds(off, A_KTILE), :], off), 1.0, 0.0)
        acc = lax.fori_loop(0, n_kt2, body, jnp.zeros((A_KTILE, tq), F32))
        return jnp.sum(acc, axis=0, keepdims=True)

    def count_ge(cand):
        cb = jnp.broadcast_to(cand, (A_KTILE, tq))
        return count(lambda k, off: k >= cb)

    kf = float(n_sel)
    zero = jnp.zeros((1, tq), I32)
    ans = jnp.where(count_ge(zero) >= kf, zero, jnp.int32(INT_MIN))

    def bit_body(b, ans):
        cand = ans + jnp.left_shift(jnp.int32(1), 30 - b)
        return jnp.where(count_ge(cand) >= kf, cand, ans)

    ans = lax.fori_loop(0, 31, bit_body, ans)
    thr = jnp.maximum(ans, jnp.int32(INT_MIN + 1))
    thr_k = jnp.broadcast_to(thr, (A_KTILE, tq))
    n_gt = count(lambda k, off: k > thr_k)
    n_ge = count(lambda k, off: k >= thr_k)
    need = kf - n_gt

    bound_ref[...] = jnp.full(bound_ref.shape, float(2 * seq), F32)

    @pl.when(jnp.max(n_ge) > kf)
    def _():
        def eq_before(bound):
            bb = jnp.broadcast_to(bound, (A_KTILE, tq))
            return count(lambda k, off: (k == thr_k) & ((off + kidx).astype(F32) < bb))

        nbits = max(1, (seq - 1).bit_length())

        def jb(b, bound):
            cand = bound + jnp.left_shift(jnp.int32(1), nbits - b).astype(F32)
            return jnp.where(eq_before(cand) <= need, cand, bound)

        bound_ref[...] = jnp.broadcast_to(lax.fori_loop(0, nbits + 1, jb, jnp.zeros((1, tq), F32)), bound_ref.shape)

    bound_b = jnp.broadcast_to(bound_ref[0:1, :], (A_KTILE, tq))

    qn2 = jnp.zeros((tq, 1), F32)
    for h in range(A_HEADS):
        qlat = _dot(q_ref[:, h * A_HEAD_DIM:(h + 1) * A_HEAD_DIM], wuk_ref[h]) * (A_HEAD_DIM ** -0.5)
        qlb = qlat.astype(BF16)
        ql_ref[h * tq:(h + 1) * tq, :] = qlb
        qf = qlb.astype(F32)
        qn2 = jnp.maximum(qn2, jnp.sum(qf * qf, axis=1, keepdims=True))

    def selected(key, off):
        return (key > thr_k) | ((key == thr_k) & ((off + kidx).astype(F32) < bound_b))

    logit_cap = jnp.sqrt(qn2 * kn_ref[:, 0:1]) * (1.0 + 2.0 ** -7) + 2.0 ** -20
    cap_b = jnp.broadcast_to(jnp.broadcast_to(logit_cap, (tq, LANES)).T[0:1, :], (A_KTILE, tq))
    l_ref[...] = jnp.zeros(l_ref.shape, F32)
    acc_ref[...] = jnp.zeros(acc_ref.shape, F32)

    def att_tile_capped(t, carry):
        off = pl.multiple_of(t * A_KTILE, A_KTILE)
        kv = ckvn_ref[pl.ds(off, A_KTILE), :]
        bias = jnp.where(selected(keys_ref[pl.ds(off, A_KTILE), :], off), -cap_b, NEG_BIG).T
        p = jnp.exp(_dot_nt(ql_ref[...], kv).reshape(A_HEADS, tq, A_KTILE) + bias[None])
        l_ref[...] += jnp.sum(p, axis=2, keepdims=True).reshape(A_HEADS * tq, 1)
        acc_ref[...] += _dot(p.reshape(A_HEADS * tq, A_KTILE).astype(BF16), kv)
        return carry

    lax.fori_loop(0, n_kt2, att_tile_capped, 0)

    @pl.when(jnp.logical_not(jnp.min(l_ref[...]) > SOFTMAX_MIN_SUM))
    def _():
        m_ref[...] = jnp.full(m_ref.shape, NEG_BIG, F32)
        l_ref[...] = jnp.zeros(l_ref.shape, F32)
        acc_ref[...] = jnp.zeros(acc_ref.shape, F32)

        def att_tile(t, carry):
            off = pl.multiple_of(t * A_KTILE, A_KTILE)
            kv = ckvn_ref[pl.ds(off, A_KTILE), :]
            bias = jnp.where(selected(keys_ref[pl.ds(off, A_KTILE), :], off), 0.0, NEG_BIG).T
            s = _dot_nt(ql_ref[...], kv).reshape(A_HEADS, tq, A_KTILE) + bias[None]
            m_old = m_ref[...].reshape(A_HEADS, tq, 1)
            m_new = jnp.maximum(m_old, jnp.max(s, axis=2, keepdims=True))
            p = jnp.exp(s - m_new)
            alpha = jnp.exp(m_old - m_new)
            l_ref[...] = (alpha * l_ref[...].reshape(A_HEADS, tq, 1)
                          + jnp.sum(p, axis=2, keepdims=True)).reshape(A_HEADS * tq, 1)
            m_ref[...] = m_new.reshape(A_HEADS * tq, 1)
            pv = _dot(p.reshape(A_HEADS * tq, A_KTILE).astype(BF16), kv)
            acc_ref[...] = alpha.reshape(A_HEADS * tq, 1) * acc_ref[...] + pv
            return carry

        lax.fori_loop(0, n_kt2, att_tile, 0)

    outs = []
    for h in range(A_HEADS):
        rs = slice(h * tq, (h + 1) * tq)
        o_lat = acc_ref[rs, :] / l_ref[rs, :]
        outs.append(_dot(o_lat.astype(BF16), wuv_ref[h]))
    att = jnp.concatenate(outs, axis=1)
    o_ref[...] = x_ref[...] + gm_ref[0] * _dot(att.astype(BF16), wout_ref[...])


def _dsa(proj, x2, gate, kv_gain, wuk_t, wuv, wout, batch, seq):
    n, d = x2.shape
    nq = seq // A_QBLOCK
    hr = A_HEADS * A_HEAD_DIM
    n_sel = min(A_TOPK_MAX, seq // 4)
    return pl.pallas_call(
        functools.partial(_dsa_kernel, n_sel=n_sel),
        grid=(batch, nq),
        in_specs=[pl.BlockSpec((A_QBLOCK, hr), lambda b, i: (b * nq + i, 0)),
                  pl.BlockSpec((A_QBLOCK, 512), lambda b, i: (b * nq + i, 2)),
                  pl.BlockSpec((A_QBLOCK, LANES), lambda b, i: (b * nq + i, 14)),
                  pl.BlockSpec((seq, A_KV_RANK), lambda b, i: (b, 6)),
                  pl.BlockSpec((seq, LANES), lambda b, i: (b, 14)),
                  pl.BlockSpec((1, A_KV_RANK), lambda b, i: (0, 0)),
                  pl.BlockSpec((A_HEADS, A_HEAD_DIM, A_KV_RANK), lambda b, i: (0, 0, 0)),
                  pl.BlockSpec((A_HEADS, A_KV_RANK, A_HEAD_DIM), lambda b, i: (0, 0, 0)),
                  pl.BlockSpec((hr, d), lambda b, i: (0, 0)),
                  pl.BlockSpec((A_QBLOCK, d), lambda b, i: (b * nq + i, 0)),
                  pl.BlockSpec((1, 1, d), lambda b, i: (b, 0, 0))],
        out_specs=pl.BlockSpec((A_QBLOCK, d), lambda b, i: (b * nq + i, 0)),
        out_shape=jax.ShapeDtypeStruct((n, d), F32),
        scratch_shapes=[pltpu.VMEM((seq, A_KV_RANK), BF16),
                        pltpu.VMEM((1, LANES), F32),
                        pltpu.VMEM((seq, A_QBLOCK), I32),
                        pltpu.VMEM((SUBLANES, A_QBLOCK), F32),
                        pltpu.VMEM((A_HEADS * A_QBLOCK, A_KV_RANK), BF16),
                        pltpu.VMEM((A_HEADS * A_QBLOCK, 1), F32),
                        pltpu.VMEM((A_HEADS * A_QBLOCK, 1), F32),
                        pltpu.VMEM((A_HEADS * A_QBLOCK, A_KV_RANK), F32)],
        compiler_params=_params(("arbitrary", "arbitrary")),
        name="dsa_attention",
    )(proj, proj, proj, proj, proj, kv_gain, wuk_t, wuv, wout, x2, gate)


def _gdn_gate_kernel(g_ref, alog_ref, dtb_ref, nat_ref, tr_ref):
    g = g_ref[...]
    col = lax.broadcasted_iota(I32, g.shape, 1)
    beta = jax.nn.sigmoid(g)
    z = g + dtb_ref[...]
    softplus = jnp.maximum(z, 0.0) + jnp.log1p(jnp.exp(-jnp.abs(z)))
    gd = -jnp.exp(alog_ref[...]) * softplus
    r = lax.broadcasted_iota(I32, (B_SUPER, B_SUPER), 0)
    c = lax.broadcasted_iota(I32, (B_SUPER, B_SUPER), 1)
    same = (r >> 6) == (c >> 6)
    tril = jnp.where(same & (c <= r), 1.0, 0.0).astype(F32)
    gc = jnp.dot(tril, gd, preferred_element_type=F32, precision=HIGHEST)
    nat = jnp.where(col < B_V_HEADS, beta, gc)
    nat_ref[...] = nat
    tr_ref[0, 0] = nat.T


def _gdn_gates(gates, a_log, dt_bias, batch, seq):
    n = gates.shape[0]
    ns = seq // B_SUPER
    pad = lambda v: jnp.zeros((1, LANES), F32).at[0, B_V_HEADS:2 * B_V_HEADS].set(v.astype(F32))
    return pl.pallas_call(
        _gdn_gate_kernel,
        grid=(batch, ns),
        in_specs=[pl.BlockSpec((B_SUPER, LANES), lambda b, s: (b * ns + s, 0)),
                  pl.BlockSpec((1, LANES), lambda b, s: (0, 0)),
                  pl.BlockSpec((1, LANES), lambda b, s: (0, 0))],
        out_specs=[pl.BlockSpec((B_SUPER, LANES), lambda b, s: (b * ns + s, 0)),
                   pl.BlockSpec((1, 1, LANES, B_SUPER), lambda b, s: (b, s, 0, 0))],
        out_shape=[jax.ShapeDtypeStruct((n, LANES), F32),
                   jax.ShapeDtypeStruct((batch, ns, LANES, B_SUPER), F32)],
        compiler_params=_params(("arbitrary", "arbitrary")),
        name="gdn_gates",
    )(gates, pad(a_log), pad(dt_bias))


def _conv_silu(raw_ref, w_ref, pad_ref):
    t = raw_ref.shape[0]
    pad_ref[0:8, :] = jnp.zeros((8, LANES), F32)
    pad_ref[8:8 + t, :] = raw_ref[...].astype(F32)
    w = w_ref[...]
    y = w[3:4] * pad_ref[8:8 + t, :]
    for j in range(B_CONV - 1):
        y = y + w[j:j + 1] * pad_ref[5 + j:5 + j + t, :]
    return y * jax.nn.sigmoid(y)


def _l2n(y):
    return y * lax.rsqrt(jnp.sum(y * y, axis=-1, keepdims=True) + RMS_EPS)


def _gdn_kernel(q_ref, k_ref, v0_ref, v1_ref, z_ref, wq_ref, wk_ref, wv0_ref, wv1_ref,
                gnat_ref, gtr_ref, gain_ref, o_ref,
                pad_ref, qn_ref, kn_ref, vs_ref, u_ref, wqg_ref, kg_ref, at_ref, el_ref, oo_ref):
    hk = pl.program_id(1)
    seq = q_ref.shape[0]
    ns = seq // B_SUPER
    nc = seq // CHUNK

    qn_ref[...] = _l2n(_conv_silu(q_ref, wq_ref, pad_ref)) * (B_HEAD_DIM ** -0.5)
    kn_ref[...] = _l2n(_conv_silu(k_ref, wk_ref, pad_ref))
    vs_ref[0] = _conv_silu(v0_ref, wv0_ref, pad_ref)
    vs_ref[1] = _conv_silu(v1_ref, wv1_ref, pad_ref)

    r = lax.broadcasted_iota(I32, (B_SUPER, B_SUPER), 0)
    c = lax.broadcasted_iota(I32, (B_SUPER, B_SUPER), 1)
    same = (r >> 6) == (c >> 6)
    tril = same & (c <= r)
    strict = same & (c < r)
    is_last = c == (r | (CHUNK - 1))
    eye = jnp.where(r == c, 1.0, 0.0).astype(F32)
    lower_left = [((r >> (level + 1)) == (c >> (level + 1))) & (((r >> level) & 1) == 1) & (((c >> level) & 1) == 0)
                  for level in range(CHUNK.bit_length() - 1)]
    lane = lax.broadcasted_iota(I32, (B_SUPER, LANES), 1)

    in_flight = min(B_GROUPS_IN_FLIGHT, ns)

    def precompute(s2, carry):
        chains = []
        for j in range(in_flight):
            s = s2 * in_flight + j
            rows = pl.ds(pl.multiple_of(s * B_SUPER, B_SUPER), B_SUPER)
            kn = kn_ref[rows, :]
            qn = qn_ref[rows, :]
            knb = kn.astype(BF16)
            kk = _dot_nt(knb, knb)
            qk = _dot_nt(qn.astype(BF16), knb)
            gnat = gnat_ref[rows, :]
            for e in range(2):
                hv = 2 * hk + e
                bcol = jnp.sum(jnp.where(lane == hv, gnat, 0.0), axis=1, keepdims=True)
                gcol = jnp.sum(jnp.where(lane == hv + B_V_HEADS, gnat, 0.0), axis=1, keepdims=True)
                grow = gtr_ref[0, s, pl.ds(hv + B_V_HEADS, 1), :]
                grow_b = jnp.broadcast_to(grow, (B_SUPER, B_SUPER))
                dec = jnp.where(tril, jnp.exp(gcol - grow_b), 0.0)
                a = jnp.where(strict, bcol * kk * dec, 0.0)
                eg = jnp.exp(gcol)
                glast = jnp.sum(jnp.where(is_last, grow_b, 0.0), axis=1, keepdims=True)
                qg = (qn * eg).astype(BF16)
                kg_ref[e, rows, :] = (kn * jnp.exp(glast - gcol)).astype(BF16)
                el_ref[e, rows, :] = jnp.exp(glast)
                attn = jnp.where(tril, qk * dec, 0.0).astype(BF16)
                for c in range(B_SUPER // CHUNK):
                    crows = slice(c * CHUNK, (c + 1) * CHUNK)
                    at_ref[e, pl.ds(pl.multiple_of(s * B_SUPER + c * CHUNK, CHUNK), CHUNK), :] = attn[crows, crows]
                    wqg_ref[e, pl.ds(pl.multiple_of((2 * s) * B_SUPER + (2 * c + 1) * CHUNK, CHUNK), CHUNK), :] = qg[crows]
                chains.append(dict(e=e, s=s, rows=rows, a=a, vb=vs_ref[e, rows, :] * bcol, kb=kn * (bcol * eg)))
        for ch in chains:
            ch["tm"] = eye
        for mask in lower_left:
            for ch in chains:
                ch["ta"] = _dot(ch["tm"].astype(BF16), jnp.where(mask, ch["a"], 0.0).astype(BF16))
            for ch in chains:
                ch["tm"] = ch["tm"] - _dot(ch["ta"].astype(BF16), ch["tm"].astype(BF16))
        for ch in chains:
            ch["resid"] = eye - _dot3(eye + ch["a"], ch["tm"])
        for ch in chains:
            ch["tm"] = ch["tm"] + _dot(ch["tm"].astype(BF16), ch["resid"].astype(BF16))
        for ch in chains:
            u_ref[ch["e"], ch["rows"], :] = _dot3(ch["tm"], ch["vb"])
            w = _dot3(ch["tm"], ch["kb"]).astype(BF16)
            for c in range(B_SUPER // CHUNK):
                wrows = pl.ds(pl.multiple_of((2 * ch["s"]) * B_SUPER + (2 * c) * CHUNK, CHUNK), CHUNK)
                wqg_ref[ch["e"], wrows, :] = w[c * CHUNK:(c + 1) * CHUNK]
        return carry

    lax.fori_loop(0, ns // in_flight, precompute, 0)

    def step(n, states):
        rows = pl.ds(pl.multiple_of(n * CHUNK, CHUNK), CHUNK)
        rows2 = pl.ds(pl.multiple_of(n * 2 * CHUNK, 2 * CHUNK), 2 * CHUNK)
        heads = range(2)
        sb = [states[e].astype(BF16) for e in heads]
        ws = [_dot(wqg_ref[e, rows2, :], sb[e]) for e in heads]
        vb = [(u_ref[e, rows, :] - ws[e][:CHUNK]).astype(BF16) for e in heads]
        for e in heads:
            oo_ref[e, rows, :] = ws[e][CHUNK:] + _dot(at_ref[e, rows, :], vb[e])
        return tuple(states[e] * el_ref[e, pl.ds(n * CHUNK, 1), :]
                     + lax.dot_general(kg_ref[e, rows, :], vb[e], _TN, preferred_element_type=F32)
                     for e in heads)

    zero = jnp.zeros((B_HEAD_DIM, B_HEAD_DIM), F32)
    lax.fori_loop(0, nc, step, (zero, zero))

    for e in range(2):
        z = z_ref[:, e * B_HEAD_DIM:(e + 1) * B_HEAD_DIM].astype(F32)
        y = _rms(oo_ref[e]) * gain_ref[...] * (z * jax.nn.sigmoid(z))
        o_ref[:, e * B_HEAD_DIM:(e + 1) * B_HEAD_DIM] = y.astype(o_ref.dtype)


def _gdn(qkvz, conv_w, gnat, gtr, out_gain, batch, seq):
    n = qkvz.shape[0]
    hd = B_HEAD_DIM
    kh = B_K_HEADS
    ns = seq // B_SUPER
    col = lambda f: (lambda b, h: (b, f(h)))
    wcol = lambda f: (lambda b, h: (0, f(h)))
    return pl.pallas_call(
        _gdn_kernel,
        grid=(batch, kh),
        in_specs=[pl.BlockSpec((seq, hd), col(lambda h: h)),
                  pl.BlockSpec((seq, hd), col(lambda h: kh + h)),
                  pl.BlockSpec((seq, hd), col(lambda h: 2 * kh + 2 * h)),
                  pl.BlockSpec((seq, hd), col(lambda h: 2 * kh + 2 * h + 1)),
                  pl.BlockSpec((seq, 2 * hd), col(lambda h: 2 * kh + h)),
                  pl.BlockSpec((B_CONV, hd), wcol(lambda h: h)),
                  pl.BlockSpec((B_CONV, hd), wcol(lambda h: kh + h)),
                  pl.BlockSpec((B_CONV, hd), wcol(lambda h: 2 * kh + 2 * h)),
                  pl.BlockSpec((B_CONV, hd), wcol(lambda h: 2 * kh + 2 * h + 1)),
                  pl.BlockSpec((seq, LANES), lambda b, h: (b, 0)),
                  pl.BlockSpec((1, ns, LANES, B_SUPER), lambda b, h: (b, 0, 0, 0)),
                  pl.BlockSpec((1, hd), lambda b, h: (0, 0))],
        out_specs=pl.BlockSpec((seq, 2 * hd), lambda b, h: (b, h)),
        out_shape=jax.ShapeDtypeStruct((n, B_V_HEADS * hd), BF16),
        scratch_shapes=[pltpu.VMEM((seq + 8, hd), F32),
                        pltpu.VMEM((seq, hd), F32),
                        pltpu.VMEM((seq, hd), F32),
                        pltpu.VMEM((2, seq, hd), F32),
                        pltpu.VMEM((2, seq, hd), F32),
                        pltpu.VMEM((2, 2 * seq, hd), BF16),
                        pltpu.VMEM((2, seq, hd), BF16),
                        pltpu.VMEM((2, seq, CHUNK), BF16),
                        pltpu.VMEM((2, seq, 1), F32),
                        pltpu.VMEM((2, seq, hd), F32)],
        compiler_params=_params(("arbitrary", "arbitrary")),
        name="gated_deltanet",
    )(qkvz, qkvz, qkvz, qkvz, qkvz, conv_w, conv_w, conv_w, conv_w, gnat, gtr, out_gain)


_CAND_W = [P_TOPK // (r + 1) for r in range(P_TOPK)]
_CAND_N = sum(_CAND_W)
_CAND_PAD = -(-_CAND_N // 8) * 8


def _top16_ranks(s, iota_k=None):
    rank = jnp.full(s.shape, float(P_TOPK), F32)
    vals = []
    for r in range(P_TOPK):
        m = jnp.max(s, axis=0, keepdims=True)
        hit = s == m
        if iota_k is not None:
            hit = iota_k == jnp.min(jnp.where(hit, iota_k, P_N_KEYS), axis=0, keepdims=True)
        rank = jnp.where(hit, float(r), rank)
        s = jnp.where(hit, -jnp.inf, s)
        vals.append(m)
    return vals, rank


def _route_head(s1, s2, iota_k, iota_k2):
    tn = s1.shape[1]
    v1, rank1 = _top16_ranks(s1, iota_k)
    v2, rank2 = _top16_ranks(s2, iota_k2)
    v2s = jnp.concatenate(v2, axis=0)
    ea = [jnp.exp(v1[r] - v1[0]) for r in range(P_TOPK)]
    eb = jnp.exp(v2s - v2[0])
    neg_pad = jnp.full((_CAND_PAD - _CAND_N, tn), -jnp.inf, F32)
    iota_c = lax.broadcasted_iota(I32, (_CAND_PAD, tn), 0)
    cand = jnp.concatenate([v1[r] + v2s[0:_CAND_W[r]] for r in range(P_TOPK)] + [neg_pad], axis=0)
    gate = jnp.concatenate([ea[r] * eb[0:_CAND_W[r]] for r in range(P_TOPK)] + [jnp.zeros_like(neg_pad)], axis=0)
    sel = jnp.zeros((_CAND_PAD, tn), F32)
    work = cand
    for _ in range(P_TOPK):
        m = jnp.max(work, axis=0, keepdims=True)
        idx = jnp.min(jnp.where(work == m, iota_c, _CAND_PAD), axis=0, keepdims=True)
        hit = iota_c == idx
        sel = jnp.where(hit, 1.0, sel)
        work = jnp.where(hit, -jnp.inf, work)
    inv_z = 1.0 / jnp.sum(sel * gate, axis=0, keepdims=True)
    cnt_t = jnp.zeros(s1.shape, F32)
    off = 0
    for r in range(P_TOPK):
        n_r = jnp.sum(sel[off:off + _CAND_W[r]], axis=0, keepdims=True)
        off += _CAND_W[r]
        cnt_t = jnp.where(rank1 == float(r), n_r, cnt_t)
    a_t = jnp.where(rank1 < float(P_TOPK), jnp.exp(s1 - v1[0]), 0.0)
    b_t = jnp.where(rank2 < float(P_TOPK), jnp.exp(s2 - v2[0]) * (0.5 * inv_z), 0.0)
    n_ranked = jnp.sum(jnp.where(rank1 < float(P_TOPK), 1.0, 0.0) + jnp.where(rank2 < float(P_TOPK), 1.0, 0.0),
                       axis=0, keepdims=True)
    return rank2, cnt_t, a_t, b_t, n_ranked


def _bf16_pair_words(lo, hi):
    bits = lambda v: lax.bitcast_convert_type(v.astype(BF16).astype(F32), I32)
    return bits(hi) | lax.shift_right_logical(bits(lo), 16)


def _peer_route_kernel(x_ref, sh_ref, sc_ref, wq_ref, sk_ref, ht_ref, rank_ref, cnt_ref, a_ref, b_ref):
    tn = x_ref.shape[0]
    h = _rms(x_ref[...]) * (1.0 + sc_ref[0]) + sh_ref[0]
    ht_ref[...] = h.T.astype(BF16)
    qb = _dot(h.astype(BF16), wq_ref[...]).astype(BF16)
    iota_k = lax.broadcasted_iota(I32, (P_N_KEYS, LANES), 0)
    half = P_N_KEYS // 2
    iota_k2 = jnp.where(iota_k < half, 2 * iota_k, 2 * (iota_k - half) + 1)
    for hd in range(P_HEADS):
        base = hd * 2 * P_HALF
        s1 = _dot_nt(sk_ref[hd, 0], qb[:, base:base + P_HALF])
        s2 = _dot_nt(sk_ref[hd, 1], qb[:, base + P_HALF:base + 2 * P_HALF])

        def emit(index_ties):
            most_ranked = jnp.zeros((), F32)
            for t in range(tn // LANES):
                lanes = slice(t * LANES, (t + 1) * LANES)
                rank2, cnt_t, a_t, b_t, n_ranked = _route_head(s1[:, lanes], s2[:, lanes],
                                                               iota_k if index_ties else None,
                                                               iota_k2 if index_ties else None)
                rank_ref[hd, :, lanes] = _bf16_pair_words(rank2[:half], rank2[half:])
                cnt_ref[hd, :, lanes] = _bf16_pair_words(cnt_t, cnt_t)
                a_ref[hd, :, lanes] = _bf16_pair_words(a_t, a_t)
                b_ref[hd, :, lanes] = _bf16_pair_words(b_t[:half], b_t[half:])
                most_ranked = jnp.maximum(most_ranked, jnp.max(n_ranked))
            return most_ranked

        most_ranked = emit(False)

        @pl.when(most_ranked > float(2 * P_TOPK))
        def _():
            emit(True)


def _peer_route(x2, shift, scale, wq, sub_keys, seq, tn):
    n, d = x2.shape
    per_b = seq // tn
    tok = lambda i: (0, 0, i)
    arr = jax.ShapeDtypeStruct((P_HEADS, P_N_KEYS, n), I32)
    arr_pair = jax.ShapeDtypeStruct((P_HEADS, P_N_KEYS // 2, n), I32)
    return pl.pallas_call(
        _peer_route_kernel,
        grid=(n // tn,),
        in_specs=[pl.BlockSpec((tn, d), lambda i: (i, 0)),
                  pl.BlockSpec((1, 1, d), lambda i: (i // per_b, 0, 0)),
                  pl.BlockSpec((1, 1, d), lambda i: (i // per_b, 0, 0)),
                  pl.BlockSpec(wq.shape, lambda i: (0, 0)),
                  pl.BlockSpec(sub_keys.shape, lambda i: (0, 0, 0, 0))],
        out_specs=[pl.BlockSpec((d, tn), lambda i: (0, i)),
                   pl.BlockSpec((P_HEADS, P_N_KEYS // 2, tn), tok),
                   pl.BlockSpec((P_HEADS, P_N_KEYS, tn), tok),
                   pl.BlockSpec((P_HEADS, P_N_KEYS, tn), tok),
                   pl.BlockSpec((P_HEADS, P_N_KEYS // 2, tn), tok)],
        out_shape=[jax.ShapeDtypeStruct((d, n), BF16), arr_pair, arr, arr, arr_pair],
        compiler_params=_params(("arbitrary",)),
        name="peer_route",
    )(x2, shift, scale, wq, sub_keys)


_PAIR_ORDER = tuple(range(0, P_N_KEYS, 2)) + tuple(range(1, P_N_KEYS, 2))


def _peer_dense_kernel(ht_ref, u_ref, vt_ref, rank_ref, cnt_ref, a_ref, b_ref, x_ref, gf_ref, fn_ref,
                       o_ref, acc_ref, act_ref, p_ref, g_ref, *, final):
    e = pl.program_id(1)
    eb = u_ref.shape[0]
    n_i = eb // P_N_KEYS

    @pl.when(e == 0)
    def _():
        acc_ref[...] = jnp.zeros(acc_ref.shape, F32)

    tn = ht_ref.shape[1]
    half = P_N_KEYS // 2
    n_slab = half // SUBLANES
    zero = jnp.zeros((2 * SUBLANES, LANES), BF16)

    def words(ref, hd, rows, lanes):
        return pltpu.bitcast(ref[hd, rows, lanes], BF16)

    def row_tile(ref, hd, ii, lanes):
        return pltpu.bitcast(jnp.broadcast_to(ref[hd, 0, ii:ii + 1, lanes], (SUBLANES, LANES)), BF16)

    def gate_build(i_rows):
        for t in range(tn // LANES):
            lanes = slice(t * LANES, (t + 1) * LANES)
            for s0 in range(0, n_slab, P_SLAB_GROUP):
                slabs = range(s0, s0 + P_SLAB_GROUP)
                g = {(ii, s): zero for ii in i_rows for s in slabs}
                for hd in range(P_HEADS):
                    cnt = {ii: row_tile(cnt_ref, hd, ii, lanes) for ii in i_rows}
                    a = {ii: row_tile(a_ref, hd, ii, lanes) for ii in i_rows}
                    for s in slabs:
                        r = words(rank_ref, hd, slice(s * SUBLANES, (s + 1) * SUBLANES), lanes)
                        b = words(b_ref, hd, slice(s * SUBLANES, (s + 1) * SUBLANES), lanes)
                        for ii in i_rows:
                            g[ii, s] = g[ii, s] + a[ii] * jnp.where(r < cnt[ii], b, zero)
                for ii in i_rows:
                    for s in slabs:
                        wrows = slice(ii * half + s * SUBLANES, ii * half + (s + 1) * SUBLANES)
                        g_ref[wrows, lanes] = pltpu.bitcast(g[ii, s], I32)

    def apply_act(i_rows):
        for ii in i_rows:
            g = pltpu.bitcast(g_ref[ii * half:(ii + 1) * half, :], BF16)
            rows = slice(ii * P_N_KEYS, (ii + 1) * P_N_KEYS)
            x = act_ref[rows, :].astype(BF16)
            p_ref[rows, :] = g * (x * (1.0 + lax.erf(x * (2.0 ** -0.5))))

    subs = [range(i0, i0 + P_SUB_ROWS) for i0 in range(0, n_i, P_SUB_ROWS)]
    rows_of = lambda sub: slice(sub[0] * P_N_KEYS, (sub[-1] + 1) * P_N_KEYS)
    gate_build(subs[0])
    for sub in subs:
        act_ref[rows_of(sub), :] = _dot(u_ref[rows_of(sub), :], ht_ref[...])
    for k, sub in enumerate(subs):
        apply_act(sub)
        if k + 1 < len(subs):
            gate_build(subs[k + 1])
        acc_ref[...] += _dot(vt_ref[:, rows_of(sub)], p_ref[rows_of(sub), :])

    @pl.when(e == pl.num_programs(1) - 1)
    def _():
        xn = x_ref[...] + gf_ref[0] * acc_ref[...].T
        if final:
            xn = _rms(xn) * fn_ref[...]
        o_ref[...] = xn


def _peer_dense(ht, u, vt, rank, cnt, a, b, x2, gate, fgain, seq, tn, eb, final):
    n, d = x2.shape
    n_e = u.shape[0]
    per_b = seq // tn
    tok = lambda t, e: (0, 0, t)
    n_i = eb // P_N_KEYS
    rows_of = lambda t, e: (0, e, 0, t)
    by_rows = lambda v: v.reshape(P_HEADS, P_N_KEYS // n_i, n_i, n)
    return pl.pallas_call(
        functools.partial(_peer_dense_kernel, final=final),
        grid=(n // tn, n_e // eb),
        in_specs=[pl.BlockSpec((d, tn), lambda t, e: (0, t)),
                  pl.BlockSpec((eb, d), lambda t, e: (e, 0)),
                  pl.BlockSpec((d, eb), lambda t, e: (0, e)),
                  pl.BlockSpec((P_HEADS, P_N_KEYS // 2, tn), tok),
                  pl.BlockSpec((P_HEADS, 1, n_i, tn), rows_of),
                  pl.BlockSpec((P_HEADS, 1, n_i, tn), rows_of),
                  pl.BlockSpec((P_HEADS, P_N_KEYS // 2, tn), tok),
                  pl.BlockSpec((tn, d), lambda t, e: (t, 0)),
                  pl.BlockSpec((1, 1, d), lambda t, e: (t // per_b, 0, 0)),
                  pl.BlockSpec((1, d), lambda t, e: (0, 0))],
        out_specs=pl.BlockSpec((tn, d), lambda t, e: (t, 0)),
        out_shape=jax.ShapeDtypeStruct((n, d), F32),
        scratch_shapes=[pltpu.VMEM((d, tn), F32),
                        pltpu.VMEM((eb, tn), F32),
                        pltpu.VMEM((eb, tn), BF16),
                        pltpu.VMEM((eb // 2, tn), I32)],
        compiler_params=_params(("arbitrary", "arbitrary")),
        name="peer_dense",
    )(ht, u, vt, rank, by_rows(cnt), by_rows(a), b, x2, gate, fgain)


def _peer(x2, shift, scale, gate, w_q, sub_keys, u_tab, v_tab, fgain, seq, final):
    tn = min(512, seq)
    sub_keys = sub_keys.at[:, 1].set(sub_keys[:, 1][:, _PAIR_ORDER])
    ht, rank, cnt, a, b = _peer_route(x2, shift, scale, w_q.astype(BF16), sub_keys.astype(BF16), seq,
                                      min(256, seq))
    return _peer_dense(ht, u_tab.astype(BF16), v_tab.astype(BF16).T, rank, cnt, a, b, x2, gate, fgain,
                       seq, tn, P_EXPERT_BLOCK, final)


def kernel(x, c, a_w_in, a_kv_norm, a_w_uk, a_w_uv, a_w_out, b_w_in, b_conv, b_a_log, b_dt_bias,
           b_out_norm, b_w_out, p_w_q, p_sub_keys, p_u, p_v, ada_w, ada_b, final_norm):
    batch, seq, d = x.shape
    n = batch * seq
    depth = ada_w.shape[0]
    x2 = x.reshape(n, d)
    mod = _adaln(c, ada_w, ada_b).reshape(depth, batch, 6, 1, d)
    fgain = final_norm.reshape(1, d)
    tm = min(512, seq)
    for layer in range(depth):
        sh_m, sc_m, g_m, sh_f, sc_f, g_f = (mod[layer, :, k] for k in range(6))
        j = layer // 2
        if layer % 2 == 0:
            w_in = a_w_in[j]
            hr = A_HEADS * A_HEAD_DIM
            c0, c1, c2 = hr + A_KV_RANK, hr + A_KV_RANK + 512, hr + A_KV_RANK + 512 + A_IDX_DIM + A_IDX_HEADS
            w_a = jnp.concatenate([w_in[:, :hr], w_in[:, c0:c1], w_in[:, hr:c0], w_in[:, c1:c2],
                                   jnp.zeros((d, 1920 - c2), w_in.dtype)], axis=1).astype(BF16)
            proj = _norm_mod_matmul(x2, sh_m, sc_m, w_a, BF16, seq, tm, 1920)
            x2 = _dsa(proj, x2, g_m, a_kv_norm[j].reshape(1, -1),
                      a_w_uk[j].transpose(1, 2, 0).astype(BF16), a_w_uv[j].transpose(1, 0, 2).astype(BF16),
                      a_w_out[j].astype(BF16), batch, seq)
        else:
            w_in = b_w_in[j]
            nqkvz = 2 * B_K_HEADS * B_HEAD_DIM + 2 * B_V_HEADS * B_HEAD_DIM
            w_g = jnp.concatenate([w_in[:, nqkvz:], jnp.zeros((d, LANES - 2 * B_V_HEADS), w_in.dtype)], axis=1)
            qkvz = _norm_mod_matmul(x2, sh_m, sc_m, w_in[:, :nqkvz].astype(BF16), BF16, seq, tm, 2048)
            gates = _norm_mod_matmul(x2, sh_m, sc_m, w_g.astype(BF16), F32, seq, tm, LANES)
            gnat, gtr = _gdn_gates(gates, b_a_log[j], b_dt_bias[j], batch, seq)
            onorm = _gdn(qkvz, b_conv[j].reshape(B_CONV, -1), gnat, gtr, b_out_norm[j].reshape(1, -1), batch, seq)
            x2 = _matmul_residual(onorm, b_w_out[j].astype(BF16), x2, g_m, seq, tm)
        x2 = _peer(x2, sh_f, sc_f, g_f, p_w_q[layer], p_sub_keys[layer], p_u[layer], p_v[layer], fgain, seq,
                   final=(layer == depth - 1))
    return x2.reshape(batch, seq, d)
```

```python
import functools

import jax
import jax.numpy as jnp
from jax import lax
from jax.experimental import pallas as pl
from jax.experimental.pallas import tpu as pltpu

F32 = jnp.float32
BF16 = jnp.bfloat16
I32 = jnp.int32
HIGHEST = lax.Precision.HIGHEST

RMS_EPS = 1e-6
CHUNK = 64
LANES = 128
SUBLANES = 8
NEG_BIG = -1e30
SOFTMAX_MIN_SUM = 1e-30

A_HEADS = 16
A_HEAD_DIM = 64
A_KV_RANK = 256
A_IDX_HEADS = 8
A_IDX_DIM = 64
A_TOPK_MAX = 256
A_QBLOCK = 256
A_KTILE = 256

B_K_HEADS = 8
B_V_HEADS = 16
B_HEAD_DIM = 128
B_CONV = 4
B_SUPER = 256
B_GROUPS_IN_FLIGHT = 4

P_HEADS = 8
P_N_KEYS = 128
P_HALF = 128
P_TOPK = 16
P_TOKEN_TILE = 512
P_EXPERT_BLOCK = 1024
P_SUB_ROWS = 4
P_SLAB_GROUP = 4

INT_MIN = int(jnp.iinfo(jnp.int32).min)

_NT = (((1,), (1,)), ((), ()))
_TN = (((0,), (0,)), ((), ()))


def _vmem_limit(mib):
    return pltpu.CompilerParams(vmem_limit_bytes=mib * 1024 * 1024)


def _params(sem, mib=48):
    return pltpu.CompilerParams(dimension_semantics=sem, vmem_limit_bytes=mib * 1024 * 1024)


def _dot(a, b):
    return jnp.dot(a, b, preferred_element_type=F32)


def _dot_nt(a, b):
    return lax.dot_general(a, b, _NT, preferred_element_type=F32)


def _split_bf16(a):
    hi = a.astype(BF16)
    lo = (a - hi.astype(F32)).astype(BF16)
    return hi, lo


def _dot3(a, b):
    ah, al = _split_bf16(a)
    bh, bl = _split_bf16(b)
    return _dot(ah, bh) + (_dot(ah, bl) + _dot(al, bh))


def _rms(x):
    return x * lax.rsqrt(jnp.mean(x * x, axis=-1, keepdims=True) + RMS_EPS)


def _adaln_kernel(c_ref, w_ref, b_ref, o_ref):
    c = c_ref[...]
    cond = c * jax.nn.sigmoid(c)
    o_ref[0] = jnp.dot(cond, w_ref[0], preferred_element_type=F32, precision=HIGHEST) + b_ref[0]


def _adaln(c, ada_w, ada_b):
    depth, d, d6 = ada_w.shape
    b = c.shape[0]
    tn = 1536
    return pl.pallas_call(
        _adaln_kernel,
        grid=(depth, d6 // tn),
        in_specs=[pl.BlockSpec((b, d), lambda l, j: (0, 0)),
                  pl.BlockSpec((1, d, tn), lambda l, j: (l, 0, j)),
                  pl.BlockSpec((1, 1, tn), lambda l, j: (l, 0, j))],
        out_specs=pl.BlockSpec((1, b, tn), lambda l, j: (l, 0, j)),
        out_shape=jax.ShapeDtypeStruct((depth, b, d6), F32),
        compiler_params=_params(("arbitrary", "arbitrary")),
        name="adaln",
    )(c, ada_w, ada_b.reshape(depth, 1, d6))


def _nmm_kernel(x_ref, sh_ref, sc_ref, w_ref, o_ref, h_ref):
    @pl.when(pl.program_id(1) == 0)
    def _():
        h = _rms(x_ref[...]) * (1.0 + sc_ref[0]) + sh_ref[0]
        h_ref[...] = h.astype(BF16)

    o_ref[...] = _dot(h_ref[...], w_ref[...]).astype(o_ref.dtype)


def _norm_mod_matmul(x2, shift, scale, w, out_dtype, seq, tm, tn):
    n, d = x2.shape
    n_out = w.shape[1]
    per_b = seq // tm
    return pl.pallas_call(
        _nmm_kernel,
        grid=(n // tm, n_out // tn),
        in_specs=[pl.BlockSpec((tm, d), lambda i, j: (i, 0)),
                  pl.BlockSpec((1, 1, d), lambda i, j: (i // per_b, 0, 0)),
                  pl.BlockSpec((1, 1, d), lambda i, j: (i // per_b, 0, 0)),
                  pl.BlockSpec((d, tn), lambda i, j: (0, j))],
        out_specs=pl.BlockSpec((tm, tn), lambda i, j: (i, j)),
        out_shape=jax.ShapeDtypeStruct((n, n_out), out_dtype),
        scratch_shapes=[pltpu.VMEM((tm, d), BF16)],
        compiler_params=_params(("arbitrary", "arbitrary")),
        name="norm_mod_matmul",
    )(x2, shift, scale, w)


def _mmres_kernel(a_ref, w_ref, x_ref, g_ref, o_ref):
    o_ref[...] = x_ref[...] + g_ref[0] * _dot(a_ref[...], w_ref[...])


def _matmul_residual(a, w, x2, gate, seq, tm):
    n, k = a.shape
    d = w.shape[1]
    per_b = seq // tm
    return pl.pallas_call(
        _mmres_kernel,
        grid=(n // tm,),
        in_specs=[pl.BlockSpec((tm, k), lambda i: (i, 0)),
                  pl.BlockSpec((k, d), lambda i: (0, 0)),
                  pl.BlockSpec((tm, d), lambda i: (i, 0)),
                  pl.BlockSpec((1, 1, d), lambda i: (i // per_b, 0, 0))],
        out_specs=pl.BlockSpec((tm, d), lambda i: (i, 0)),
        out_shape=jax.ShapeDtypeStruct((n, d), F32),
        compiler_params=_params(("arbitrary",)),
        name="matmul_residual",
    )(a, w, x2, gate)


def _dsa_kernel(q_ref, qidx_ref, kwq_ref, ckv_ref, kw_ref, gain_ref, wuk_ref, wuv_ref, wout_ref,
                x_ref, gm_ref, o_ref,
                ckvn_ref, kn_ref, keys_ref, bound_ref, ql_ref, m_ref, l_ref, acc_ref, *, n_sel):
    qi = pl.program_id(1)
    tq = A_QBLOCK
    n_kt = (qi + 1) * (tq // LANES)
    seq = keys_ref.shape[0]

    @pl.when(qi == 0)
    def _():
        cn = (_rms(ckv_ref[...].astype(F32)) * gain_ref[...]).astype(BF16)
        ckvn_ref[...] = cn
        cf = cn.astype(F32)
        kn2 = jnp.max(jnp.sum(cf * cf, axis=1, keepdims=True), axis=0, keepdims=True)
        kn_ref[...] = jnp.broadcast_to(kn2, kn_ref.shape)

    krow = lax.broadcasted_iota(I32, (LANES, tq), 0)
    qlane = lax.broadcasted_iota(I32, (LANES, tq), 1)
    qh = [qidx_ref[:, h * A_IDX_DIM:(h + 1) * A_IDX_DIM] for h in range(A_IDX_HEADS)]
    w_t = kwq_ref[...].astype(F32).T * ((A_IDX_HEADS ** -0.5) * (A_IDX_DIM ** -0.5))
    wrow = [w_t[A_IDX_DIM + h:A_IDX_DIM + h + 1, :] for h in range(A_IDX_HEADS)]
    q_chunk = qi * (tq // CHUNK) + (qlane >> 6)
    k_chunk_in_tile = krow >> 6

    n_kt2 = (n_kt + 1) // 2

    def score_pair(t, carry):
        kts = [2 * t, 2 * t + 1]
        offs = [pl.multiple_of(kt * LANES, LANES) for kt in kts]
        kt_k = [kw_ref[pl.ds(off, LANES), 0:A_IDX_DIM] for off in offs]
        sc = [jnp.zeros((LANES, tq), F32) for _ in kts]
        for h in range(A_IDX_HEADS):
            for j in range(2):
                sc[j] = sc[j] + wrow[h] * jnp.maximum(_dot_nt(kt_k[j], qh[h]), 0.0)
        for j in range(2):
            bits = lax.bitcast_convert_type(sc[j], I32)
            key = jnp.where(bits < 0, bits ^ jnp.int32(0x7FFFFFFF), bits)
            inadm = (kts[j] * (LANES // CHUNK) + k_chunk_in_tile) > q_chunk
            keys_ref[pl.ds(offs[j], LANES), :] = jnp.where(inadm, jnp.int32(INT_MIN), key)
        return carry

    lax.fori_loop(0, n_kt2, score_pair, 0)

    kidx = lax.broadcasted_iota(I32, (A_KTILE, tq), 0)

    def count(pred):
        def body(t, acc):
            off = pl.multiple_of(t * A_KTILE, A_KTILE)
            return acc + jnp.where(pred(keys_ref[pl.ds(off, A_KTILE), :], off), 1.0, 0.0)
        acc = lax.fori_loop(0, n_kt2, body, jnp.zeros((A_KTILE, tq), F32))
        return jnp.sum(acc, axis=0, keepdims=True)

    def count_ge(cand):
        cb = jnp.broadcast_to(cand, (A_KTILE, tq))
        return count(lambda k, off: k >= cb)

    kf = float(n_sel)
    zero = jnp.zeros((1, tq), I32)
    ans = jnp.where(count_ge(zero) >= kf, zero, jnp.int32(INT_MIN))

    def bit_body(b, ans):
        cand = ans + jnp.left_shift(jnp.int32(1), 30 - b)
        return jnp.where(count_ge(cand) >= kf, cand, ans)

    ans = lax.fori_loop(0, 31, bit_body, ans)
    thr = jnp.maximum(ans, jnp.int32(INT_MIN + 1))
    thr_k = jnp.broadcast_to(thr, (A_KTILE, tq))
    n_gt = count(lambda k, off: k > thr_k)
    n_ge = count(lambda k, off: k >= thr_k)
    need = kf - n_gt

    bound_ref[...] = jnp.full(bound_ref.shape, float(2 * seq), F32)

    @pl.when(jnp.max(n_ge) > kf)
    def _():
        def eq_before(bound):
            bb = jnp.broadcast_to(bound, (A_KTILE, tq))
            return count(lambda k, off: (k == thr_k) & ((off + kidx).astype(F32) < bb))

        nbits = max(1, (seq - 1).bit_length())

        def jb(b, bound):
            cand = bound + jnp.left_shift(jnp.int32(1), nbits - b).astype(F32)
            return jnp.where(eq_before(cand) <= need, cand, bound)

        bound_ref[...] = jnp.broadcast_to(lax.fori_loop(0, nbits + 1, jb, jnp.zeros((1, tq), F32)), bound_ref.shape)

    bound_b = jnp.broadcast_to(bound_ref[0:1, :], (A_KTILE, tq))

    qn2 = jnp.zeros((tq, 1), F32)
    for h in range(A_HEADS):
        qlat = _dot(q_ref[:, h * A_HEAD_DIM:(h + 1) * A_HEAD_DIM], wuk_ref[h]) * (A_HEAD_DIM ** -0.5)
        qlb = qlat.astype(BF16)
        ql_ref[h * tq:(h + 1) * tq, :] = qlb
        qf = qlb.astype(F32)
        qn2 = jnp.maximum(qn2, jnp.sum(qf * qf, axis=1, keepdims=True))

    def selected(key, off):
        return (key > thr_k) | ((key == thr_k) & ((off + kidx).astype(F32) < bound_b))

    logit_cap = jnp.sqrt(qn2 * kn_ref[:, 0:1]) * (1.0 + 2.0 ** -7) + 2.0 ** -20
    cap_b = jnp.broadcast_to(jnp.broadcast_to(logit_cap, (tq, LANES)).T[0:1, :], (A_KTILE, tq))
    l_ref[...] = jnp.zeros(l_ref.shape, F32)
    acc_ref[...] = jnp.zeros(acc_ref.shape, F32)

    def att_tile_capped(t, carry):
        off = pl.multiple_of(t * A_KTILE, A_KTILE)
        kv = ckvn_ref[pl.ds(off, A_KTILE), :]
        bias = jnp.where(selected(keys_ref[pl.ds(off, A_KTILE), :], off), -cap_b, NEG_BIG).T
        p = jnp.exp(_dot_nt(ql_ref[...], kv).reshape(A_HEADS, tq, A_KTILE) + bias[None])
        l_ref[...] += jnp.sum(p, axis=2, keepdims=True).reshape(A_HEADS * tq, 1)
        acc_ref[...] += _dot(p.reshape(A_HEADS * tq, A_KTILE).astype(BF16), kv)
        return carry

    lax.fori_loop(0, n_kt2, att_tile_capped, 0)

    @pl.when(jnp.logical_not(jnp.min(l_ref[...]) > SOFTMAX_MIN_SUM))
    def _():
        m_ref[...] = jnp.full(m_ref.shape, NEG_BIG, F32)
        l_ref[...] = jnp.zeros(l_ref.shape, F32)
        acc_ref[...] = jnp.zeros(acc_ref.shape, F32)

        def att_tile(t, carry):
            off = pl.multiple_of(t * A_KTILE, A_KTILE)
            kv = ckvn_ref[pl.ds(off, A_KTILE), :]
            bias = jnp.where(selected(keys_ref[pl.ds(off, A_KTILE), :], off), 0.0, NEG_BIG).T
            s = _dot_nt(ql_ref[...], kv).reshape(A_HEADS, tq, A_KTILE) + bias[None]
            m_old = m_ref[...].reshape(A_HEADS, tq, 1)
            m_new = jnp.maximum(m_old, jnp.max(s, axis=2, keepdims=True))
            p = jnp.exp(s - m_new)
            alpha = jnp.exp(m_old - m_new)
            l_ref[...] = (alpha * l_ref[...].reshape(A_HEADS, tq, 1)
                          + jnp.sum(p, axis=2, keepdims=True)).reshape(A_HEADS * tq, 1)
            m_ref[...] = m_new.reshape(A_HEADS * tq, 1)
            pv = _dot(p.reshape(A_HEADS * tq, A_KTILE).astype(BF16), kv)
            acc_ref[...] = alpha.reshape(A_HEADS * tq, 1) * acc_ref[...] + pv
            return carry

        lax.fori_loop(0, n_kt2, att_tile, 0)

    outs = []
    for h in range(A_HEADS):
        rs = slice(h * tq, (h + 1) * tq)
        o_lat = acc_ref[rs, :] / l_ref[rs, :]
        outs.append(_dot(o_lat.astype(BF16), wuv_ref[h]))
    att = jnp.concatenate(outs, axis=1)
    o_ref[...] = x_ref[...] + gm_ref[0] * _dot(att.astype(BF16), wout_ref[...])


def _dsa(proj, x2, gate, kv_gain, wuk_t, wuv, wout, batch, seq):
    n, d = x2.shape
    nq = seq // A_QBLOCK
    hr = A_HEADS * A_HEAD_DIM
    n_sel = min(A_TOPK_MAX, seq // 4)
    return pl.pallas_call(
        functools.partial(_dsa_kernel, n_sel=n_sel),
        grid=(batch, nq),
        in_specs=[pl.BlockSpec((A_QBLOCK, hr), lambda b, i: (b * nq + i, 0)),
                  pl.BlockSpec((A_QBLOCK, 512), lambda b, i: (b * nq + i, 2)),
                  pl.BlockSpec((A_QBLOCK, LANES), lambda b, i: (b * nq + i, 14)),
                  pl.BlockSpec((seq, A_KV_RANK), lambda b, i: (b, 6)),
                  pl.BlockSpec((seq, LANES), lambda b, i: (b, 14)),
                  pl.BlockSpec((1, A_KV_RANK), lambda b, i: (0, 0)),
                  pl.BlockSpec((A_HEADS, A_HEAD_DIM, A_KV_RANK), lambda b, i: (0, 0, 0)),
                  pl.BlockSpec((A_HEADS, A_KV_RANK, A_HEAD_DIM), lambda b, i: (0, 0, 0)),
                  pl.BlockSpec((hr, d), lambda b, i: (0, 0)),
                  pl.BlockSpec((A_QBLOCK, d), lambda b, i: (b * nq + i, 0)),
                  pl.BlockSpec((1, 1, d), lambda b, i: (b, 0, 0))],
        out_specs=pl.BlockSpec((A_QBLOCK, d), lambda b, i: (b * nq + i, 0)),
        out_shape=jax.ShapeDtypeStruct((n, d), F32),
        scratch_shapes=[pltpu.VMEM((seq, A_KV_RANK), BF16),
                        pltpu.VMEM((1, LANES), F32),
                        pltpu.VMEM((seq, A_QBLOCK), I32),
                        pltpu.VMEM((SUBLANES, A_QBLOCK), F32),
                        pltpu.VMEM((A_HEADS * A_QBLOCK, A_KV_RANK), BF16),
                        pltpu.VMEM((A_HEADS * A_QBLOCK, 1), F32),
                        pltpu.VMEM((A_HEADS * A_QBLOCK, 1), F32),
                        pltpu.VMEM((A_HEADS * A_QBLOCK, A_KV_RANK), F32)],
        compiler_params=_params(("arbitrary", "arbitrary")),
        name="dsa_attention",
    )(proj, proj, proj, proj, proj, kv_gain, wuk_t, wuv, wout, x2, gate)


def _gdn_gate_kernel(g_ref, alog_ref, dtb_ref, nat_ref, tr_ref):
    g = g_ref[...]
    col = lax.broadcasted_iota(I32, g.shape, 1)
    beta = jax.nn.sigmoid(g)
    z = g + dtb_ref[...]
    softplus = jnp.maximum(z, 0.0) + jnp.log1p(jnp.exp(-jnp.abs(z)))
    gd = -jnp.exp(alog_ref[...]) * softplus
    r = lax.broadcasted_iota(I32, (B_SUPER, B_SUPER), 0)
    c = lax.broadcasted_iota(I32, (B_SUPER, B_SUPER), 1)
    same = (r >> 6) == (c >> 6)
    tril = jnp.where(same & (c <= r), 1.0, 0.0).astype(F32)
    gc = jnp.dot(tril, gd, preferred_element_type=F32, precision=HIGHEST)
    nat = jnp.where(col < B_V_HEADS, beta, gc)
    nat_ref[...] = nat
    tr_ref[0, 0] = nat.T


def _gdn_gates(gates, a_log, dt_bias, batch, seq):
    n = gates.shape[0]
    ns = seq // B_SUPER
    pad = lambda v: jnp.zeros((1, LANES), F32).at[0, B_V_HEADS:2 * B_V_HEADS].set(v.astype(F32))
    return pl.pallas_call(
        _gdn_gate_kernel,
        grid=(batch, ns),
        in_specs=[pl.BlockSpec((B_SUPER, LANES), lambda b, s: (b * ns + s, 0)),
                  pl.BlockSpec((1, LANES), lambda b, s: (0, 0)),
                  pl.BlockSpec((1, LANES), lambda b, s: (0, 0))],
        out_specs=[pl.BlockSpec((B_SUPER, LANES), lambda b, s: (b * ns + s, 0)),
                   pl.BlockSpec((1, 1, LANES, B_SUPER), lambda b, s: (b, s, 0, 0))],
        out_shape=[jax.ShapeDtypeStruct((n, LANES), F32),
                   jax.ShapeDtypeStruct((batch, ns, LANES, B_SUPER), F32)],
        compiler_params=_params(("arbitrary", "arbitrary")),
        name="gdn_gates",
    )(gates, pad(a_log), pad(dt_bias))


def _conv_silu(raw_ref, w_ref, pad_ref):
    t = raw_ref.shape[0]
    pad_ref[0:8, :] = jnp.zeros((8, LANES), F32)
    pad_ref[8:8 + t, :] = raw_ref[...].astype(F32)
    w = w_ref[...]
    y = w[3:4] * pad_ref[8:8 + t, :]
    for j in range(B_CONV - 1):
        y = y + w[j:j + 1] * pad_ref[5 + j:5 + j + t, :]
    return y * jax.nn.sigmoid(y)


def _l2n(y):
    return y * lax.rsqrt(jnp.sum(y * y, axis=-1, keepdims=True) + RMS_EPS)


def _gdn_kernel(q_ref, k_ref, v0_ref, v1_ref, z_ref, wq_ref, wk_ref, wv0_ref, wv1_ref,
                gnat_ref, gtr_ref, gain_ref, o_ref,
                pad_ref, qn_ref, kn_ref, vs_ref, u_ref, wqg_ref, kg_ref, at_ref, el_ref, oo_ref):
    hk = pl.program_id(1)
    seq = q_ref.shape[0]
    ns = seq // B_SUPER
    nc = seq // CHUNK

    qn_ref[...] = _l2n(_conv_silu(q_ref, wq_ref, pad_ref)) * (B_HEAD_DIM ** -0.5)
    kn_ref[...] = _l2n(_conv_silu(k_ref, wk_ref, pad_ref))
    vs_ref[0] = _conv_silu(v0_ref, wv0_ref, pad_ref)
    vs_ref[1] = _conv_silu(v1_ref, wv1_ref, pad_ref)

    r = lax.broadcasted_iota(I32, (B_SUPER, B_SUPER), 0)
    c = lax.broadcasted_iota(I32, (B_SUPER, B_SUPER), 1)
    same = (r >> 6) == (c >> 6)
    tril = same & (c <= r)
    strict = same & (c < r)
    is_last = c == (r | (CHUNK - 1))
    eye = jnp.where(r == c, 1.0, 0.0).astype(F32)
    lower_left = [((r >> (level + 1)) == (c >> (level + 1))) & (((r >> level) & 1) == 1) & (((c >> level) & 1) == 0)
                  for level in range(CHUNK.bit_length() - 1)]
    lane = lax.broadcasted_iota(I32, (B_SUPER, LANES), 1)

    in_flight = min(B_GROUPS_IN_FLIGHT, ns)

    def precompute(s2, carry):
        chains = []
        for j in range(in_flight):
            s = s2 * in_flight + j
            rows = pl.ds(pl.multiple_of(s * B_SUPER, B_SUPER), B_SUPER)
            kn = kn_ref[rows, :]
            qn = qn_ref[rows, :]
            knb = kn.astype(BF16)
            kk = _dot_nt(knb, knb)
            qk = _dot_nt(qn.astype(BF16), knb)
            gnat = gnat_ref[rows, :]
            for e in range(2):
                hv = 2 * hk + e
                bcol = jnp.sum(jnp.where(lane == hv, gnat, 0.0), axis=1, keepdims=True)
                gcol = jnp.sum(jnp.where(lane == hv + B_V_HEADS, gnat, 0.0), axis=1, keepdims=True)
                grow = gtr_ref[0, s, pl.ds(hv + B_V_HEADS, 1), :]
                grow_b = jnp.broadcast_to(grow, (B_SUPER, B_SUPER))
                dec = jnp.where(tril, jnp.exp(gcol - grow_b), 0.0)
                a = jnp.where(strict, bcol * kk * dec, 0.0)
                eg = jnp.exp(gcol)
                glast = jnp.sum(jnp.where(is_last, grow_b, 0.0), axis=1, keepdims=True)
                qg = (qn * eg).astype(BF16)
                kg_ref[e, rows, :] = (kn * jnp.exp(glast - gcol)).astype(BF16)
                el_ref[e, rows, :] = jnp.exp(glast)
                attn = jnp.where(tril, qk * dec, 0.0).astype(BF16)
                for c in range(B_SUPER // CHUNK):
                    crows = slice(c * CHUNK, (c + 1) * CHUNK)
                    at_ref[e, pl.ds(pl.multiple_of(s * B_SUPER + c * CHUNK, CHUNK), CHUNK), :] = attn[crows, crows]
                    wqg_ref[e, pl.ds(pl.multiple_of((2 * s) * B_SUPER + (2 * c + 1) * CHUNK, CHUNK), CHUNK), :] = qg[crows]
                chains.append(dict(e=e, s=s, rows=rows, a=a, vb=vs_ref[e, rows, :] * bcol, kb=kn * (bcol * eg)))
        for ch in chains:
            ch["tm"] = eye
        for mask in lower_left:
            for ch in chains:
                ch["ta"] = _dot(ch["tm"].astype(BF16), jnp.where(mask, ch["a"], 0.0).astype(BF16))
            for ch in chains:
                ch["tm"] = ch["tm"] - _dot(ch["ta"].astype(BF16), ch["tm"].astype(BF16))
        for ch in chains:
            ch["resid"] = eye - _dot3(eye + ch["a"], ch["tm"])
        for ch in chains:
            ch["tm"] = ch["tm"] + _dot(ch["tm"].astype(BF16), ch["resid"].astype(BF16))
        for ch in chains:
            u_ref[ch["e"], ch["rows"], :] = _dot3(ch["tm"], ch["vb"])
            w = _dot3(ch["tm"], ch["kb"]).astype(BF16)
            for c in range(B_SUPER // CHUNK):
                wrows = pl.ds(pl.multiple_of((2 * ch["s"]) * B_SUPER + (2 * c) * CHUNK, CHUNK), CHUNK)
                wqg_ref[ch["e"], wrows, :] = w[c * CHUNK:(c + 1) * CHUNK]
        return carry

    lax.fori_loop(0, ns // in_flight, precompute, 0)

    def step(n, states):
        rows = pl.ds(pl.multiple_of(n * CHUNK, CHUNK), CHUNK)
        rows2 = pl.ds(pl.multiple_of(n * 2 * CHUNK, 2 * CHUNK), 2 * CHUNK)
        heads = range(2)
        sb = [states[e].astype(BF16) for e in heads]
        ws = [_dot(wqg_ref[e, rows2, :], sb[e]) for e in heads]
        vb = [(u_ref[e, rows, :] - ws[e][:CHUNK]).astype(BF16) for e in heads]
        for e in heads:
            oo_ref[e, rows, :] = ws[e][CHUNK:] + _dot(at_ref[e, rows, :], vb[e])
        return tuple(states[e] * el_ref[e, pl.ds(n * CHUNK, 1), :]
                     + lax.dot_general(kg_ref[e, rows, :], vb[e], _TN, preferred_element_type=F32)
                     for e in heads)

    zero = jnp.zeros((B_HEAD_DIM, B_HEAD_DIM), F32)
    lax.fori_loop(0, nc, step, (zero, zero))

    for e in range(2):
        z = z_ref[:, e * B_HEAD_DIM:(e + 1) * B_HEAD_DIM].astype(F32)
        y = _rms(oo_ref[e]) * gain_ref[...] * (z * jax.nn.sigmoid(z))
        o_ref[:, e * B_HEAD_DIM:(e + 1) * B_HEAD_DIM] = y.astype(o_ref.dtype)


def _gdn(qkvz, conv_w, gnat, gtr, out_gain, batch, seq):
    n = qkvz.shape[0]
    hd = B_HEAD_DIM
    kh = B_K_HEADS
    ns = seq // B_SUPER
    col = lambda f: (lambda b, h: (b, f(h)))
    wcol = lambda f: (lambda b, h: (0, f(h)))
    return pl.pallas_call(
        _gdn_kernel,
        grid=(batch, kh),
        in_specs=[pl.BlockSpec((seq, hd), col(lambda h: h)),
                  pl.BlockSpec((seq, hd), col(lambda h: kh + h)),
                  pl.BlockSpec((seq, hd), col(lambda h: 2 * kh + 2 * h)),
                  pl.BlockSpec((seq, hd), col(lambda h: 2 * kh + 2 * h + 1)),
                  pl.BlockSpec((seq, 2 * hd), col(lambda h: 2 * kh + h)),
                  pl.BlockSpec((B_CONV, hd), wcol(lambda h: h)),
                  pl.BlockSpec((B_CONV, hd), wcol(lambda h: kh + h)),
                  pl.BlockSpec((B_CONV, hd), wcol(lambda h: 2 * kh + 2 * h)),
                  pl.BlockSpec((B_CONV, hd), wcol(lambda h: 2 * kh + 2 * h + 1)),
                  pl.BlockSpec((seq, LANES), lambda b, h: (b, 0)),
                  pl.BlockSpec((1, ns, LANES, B_SUPER), lambda b, h: (b, 0, 0, 0)),
                  pl.BlockSpec((1, hd), lambda b, h: (0, 0))],
        out_specs=pl.BlockSpec((seq, 2 * hd), lambda b, h: (b, h)),
        out_shape=jax.ShapeDtypeStruct((n, B_V_HEADS * hd), BF16),
        scratch_shapes=[pltpu.VMEM((seq + 8, hd), F32),
                        pltpu.VMEM((seq, hd), F32),
                        pltpu.VMEM((seq, hd), F32),
                        pltpu.VMEM((2, seq, hd), F32),
                        pltpu.VMEM((2, seq, hd), F32),
                        pltpu.VMEM((2, 2 * seq, hd), BF16),
                        pltpu.VMEM((2, seq, hd), BF16),
                        pltpu.VMEM((2, seq, CHUNK), BF16),
                        pltpu.VMEM((2, seq, 1), F32),
                        pltpu.VMEM((2, seq, hd), F32)],
        compiler_params=_params(("arbitrary", "arbitrary")),
        name="gated_deltanet",
    )(qkvz, qkvz, qkvz, qkvz, qkvz, conv_w, conv_w, conv_w, conv_w, gnat, gtr, out_gain)


_CAND_W = [P_TOPK // (r + 1) for r in range(P_TOPK)]
_CAND_N = sum(_CAND_W)
_CAND_PAD = -(-_CAND_N // 8) * 8


def _top16_ranks(s, iota_k=None):
    rank = jnp.full(s.shape, float(P_TOPK), F32)
    vals = []
    for r in range(P_TOPK):
        m = jnp.max(s, axis=0, keepdims=True)
        hit = s == m
        if iota_k is not None:
            hit = iota_k == jnp.min(jnp.where(hit, iota_k, P_N_KEYS), axis=0, keepdims=True)
        rank = jnp.where(hit, float(r), rank)
        s = jnp.where(hit, -jnp.inf, s)
        vals.append(m)
    return vals, rank


def _route_head(s1, s2, iota_k, iota_k2):
    tn = s1.shape[1]
    v1, rank1 = _top16_ranks(s1, iota_k)
    v2, rank2 = _top16_ranks(s2, iota_k2)
    v2s = jnp.concatenate(v2, axis=0)
    ea = [jnp.exp(v1[r] - v1[0]) for r in range(P_TOPK)]
    eb = jnp.exp(v2s - v2[0])
    neg_pad = jnp.full((_CAND_PAD - _CAND_N, tn), -jnp.inf, F32)
    iota_c = lax.broadcasted_iota(I32, (_CAND_PAD, tn), 0)
    cand = jnp.concatenate([v1[r] + v2s[0:_CAND_W[r]] for r in range(P_TOPK)] + [neg_pad], axis=0)
    gate = jnp.concatenate([ea[r] * eb[0:_CAND_W[r]] for r in range(P_TOPK)] + [jnp.zeros_like(neg_pad)], axis=0)
    sel = jnp.zeros((_CAND_PAD, tn), F32)
    work = cand
    for _ in range(P_TOPK):
        m = jnp.max(work, axis=0, keepdims=True)
        idx = jnp.min(jnp.where(work == m, iota_c, _CAND_PAD), axis=0, keepdims=True)
        hit = iota_c == idx
        sel = jnp.where(hit, 1.0, sel)
        work = jnp.where(hit, -jnp.inf, work)
    inv_z = 1.0 / jnp.sum(sel * gate, axis=0, keepdims=True)
    cnt_t = jnp.zeros(s1.shape, F32)
    off = 0
    for r in range(P_TOPK):
        n_r = jnp.sum(sel[off:off + _CAND_W[r]], axis=0, keepdims=True)
        off += _CAND_W[r]
        cnt_t = jnp.where(rank1 == float(r), n_r, cnt_t)
    a_t = jnp.where(rank1 < float(P_TOPK), jnp.exp(s1 - v1[0]), 0.0)
    b_t = jnp.where(rank2 < float(P_TOPK), jnp.exp(s2 - v2[0]) * (0.5 * inv_z), 0.0)
    n_ranked = jnp.sum(jnp.where(rank1 < float(P_TOPK), 1.0, 0.0) + jnp.where(rank2 < float(P_TOPK), 1.0, 0.0),
                       axis=0, keepdims=True)
    return rank2, cnt_t, a_t, b_t, n_ranked


def _bf16_pair_words(lo, hi):
    bits = lambda v: lax.bitcast_convert_type(v.astype(BF16).astype(F32), I32)
    return bits(hi) | lax.shift_right_logical(bits(lo), 16)


def _peer_route_kernel(x_ref, sh_ref, sc_ref, wq_ref, sk_ref, ht_ref, rank_ref, cnt_ref, a_ref, b_ref):
    tn = x_ref.shape[0]
    h = _rms(x_ref[...]) * (1.0 + sc_ref[0]) + sh_ref[0]
    ht_ref[...] = h.T.astype(BF16)
    qb = _dot(h.astype(BF16), wq_ref[...]).astype(BF16)
    iota_k = lax.broadcasted_iota(I32, (P_N_KEYS, LANES), 0)
    half = P_N_KEYS // 2
    iota_k2 = jnp.where(iota_k < half, 2 * iota_k, 2 * (iota_k - half) + 1)
    for hd in range(P_HEADS):
        base = hd * 2 * P_HALF
        s1 = _dot_nt(sk_ref[hd, 0], qb[:, base:base + P_HALF])
        s2 = _dot_nt(sk_ref[hd, 1], qb[:, base + P_HALF:base + 2 * P_HALF])

        def emit(index_ties):
            most_ranked = jnp.zeros((), F32)
            for t in range(tn // LANES):
                lanes = slice(t * LANES, (t + 1) * LANES)
                rank2, cnt_t, a_t, b_t, n_ranked = _route_head(s1[:, lanes], s2[:, lanes],
                                                               iota_k if index_ties else None,
                                                               iota_k2 if index_ties else None)
                rank_ref[hd, :, lanes] = _bf16_pair_words(rank2[:half], rank2[half:])
                cnt_ref[hd, :, lanes] = _bf16_pair_words(cnt_t, cnt_t)
                a_ref[hd, :, lanes] = _bf16_pair_words(a_t, a_t)
                b_ref[hd, :, lanes] = _bf16_pair_words(b_t[:half], b_t[half:])
                most_ranked = jnp.maximum(most_ranked, jnp.max(n_ranked))
            return most_ranked

        most_ranked = emit(False)

        @pl.when(most_ranked > float(2 * P_TOPK))
        def _():
            emit(True)


def _peer_route(x2, shift, scale, wq, sub_keys, seq, tn):
    n, d = x2.shape
    per_b = seq // tn
    tok = lambda i: (0, 0, i)
    arr = jax.ShapeDtypeStruct((P_HEADS, P_N_KEYS, n), I32)
    arr_pair = jax.ShapeDtypeStruct((P_HEADS, P_N_KEYS // 2, n), I32)
    return pl.pallas_call(
        _peer_route_kernel,
        grid=(n // tn,),
        in_specs=[pl.BlockSpec((tn, d), lambda i: (i, 0)),
                  pl.BlockSpec((1, 1, d), lambda i: (i // per_b, 0, 0)),
                  pl.BlockSpec((1, 1, d), lambda i: (i // per_b, 0, 0)),
                  pl.BlockSpec(wq.shape, lambda i: (0, 0)),
                  pl.BlockSpec(sub_keys.shape, lambda i: (0, 0, 0, 0))],
        out_specs=[pl.BlockSpec((d, tn), lambda i: (0, i)),
                   pl.BlockSpec((P_HEADS, P_N_KEYS // 2, tn), tok),
                   pl.BlockSpec((P_HEADS, P_N_KEYS, tn), tok),
                   pl.BlockSpec((P_HEADS, P_N_KEYS, tn), tok),
                   pl.BlockSpec((P_HEADS, P_N_KEYS // 2, tn), tok)],
        out_shape=[jax.ShapeDtypeStruct((d, n), BF16), arr_pair, arr, arr, arr_pair],
        compiler_params=_params(("arbitrary",)),
        name="peer_route",
    )(x2, shift, scale, wq, sub_keys)


_PAIR_ORDER = tuple(range(0, P_N_KEYS, 2)) + tuple(range(1, P_N_KEYS, 2))


def _peer_dense_kernel(ht_ref, u_ref, vt_ref, rank_ref, cnt_ref, a_ref, b_ref, x_ref, gf_ref, fn_ref,
                       o_ref, acc_ref, act_ref, p_ref, g_ref, *, final):
    e = pl.program_id(1)
    eb = u_ref.shape[0]
    n_i = eb // P_N_KEYS

    @pl.when(e == 0)
    def _():
        acc_ref[...] = jnp.zeros(acc_ref.shape, F32)

    tn = ht_ref.shape[1]
    half = P_N_KEYS // 2
    n_slab = half // SUBLANES
    zero = jnp.zeros((2 * SUBLANES, LANES), BF16)

    def words(ref, hd, rows, lanes):
        return pltpu.bitcast(ref[hd, rows, lanes], BF16)

    def row_tile(ref, hd, ii, lanes):
        return pltpu.bitcast(jnp.broadcast_to(ref[hd, 0, ii:ii + 1, lanes], (SUBLANES, LANES)), BF16)

    def gate_build(i_rows):
        for t in range(tn // LANES):
            lanes = slice(t * LANES, (t + 1) * LANES)
            for s0 in range(0, n_slab, P_SLAB_GROUP):
                slabs = range(s0, s0 + P_SLAB_GROUP)
                g = {(ii, s): zero for ii in i_rows for s in slabs}
                for hd in range(P_HEADS):
                    cnt = {ii: row_tile(cnt_ref, hd, ii, lanes) for ii in i_rows}
                    a = {ii: row_tile(a_ref, hd, ii, lanes) for ii in i_rows}
                    for s in slabs:
                        r = words(rank_ref, hd, slice(s * SUBLANES, (s + 1) * SUBLANES), lanes)
                        b = words(b_ref, hd, slice(s * SUBLANES, (s + 1) * SUBLANES), lanes)
                        for ii in i_rows:
                            g[ii, s] = g[ii, s] + a[ii] * jnp.where(r < cnt[ii], b, zero)
                for ii in i_rows:
                    for s in slabs:
                        wrows = slice(ii * half + s * SUBLANES, ii * half + (s + 1) * SUBLANES)
                        g_ref[wrows, lanes] = pltpu.bitcast(g[ii, s], I32)

    def apply_act(i_rows):
        for ii in i_rows:
            g = pltpu.bitcast(g_ref[ii * half:(ii + 1) * half, :], BF16)
            rows = slice(ii * P_N_KEYS, (ii + 1) * P_N_KEYS)
            x = act_ref[rows, :].astype(BF16)
            p_ref[rows, :] = g * (x * (1.0 + lax.erf(x * (2.0 ** -0.5))))

    subs = [range(i0, i0 + P_SUB_ROWS) for i0 in range(0, n_i, P_SUB_ROWS)]
    rows_of = lambda sub: slice(sub[0] * P_N_KEYS, (sub[-1] + 1) * P_N_KEYS)
    gate_build(subs[0])
    for sub in subs:
        act_ref[rows_of(sub), :] = _dot(u_ref[rows_of(sub), :], ht_ref[...])
    for k, sub in enumerate(subs):
        apply_act(sub)
        if k + 1 < len(subs):
            gate_build(subs[k + 1])
        acc_ref[...] += _dot(vt_ref[:, rows_of(sub)], p_ref[rows_of(sub), :])

    @pl.when(e == pl.num_programs(1) - 1)
    def _():
        xn = x_ref[...] + gf_ref[0] * acc_ref[...].T
        if final:
            xn = _rms(xn) * fn_ref[...]
        o_ref[...] = xn


def _peer_dense(ht, u, vt, rank, cnt, a, b, x2, gate, fgain, seq, tn, eb, final):
    n, d = x2.shape
    n_e = u.shape[0]
    per_b = seq // tn
    tok = lambda t, e: (0, 0, t)
    n_i = eb // P_N_KEYS
    rows_of = lambda t, e: (0, e, 0, t)
    by_rows = lambda v: v.reshape(P_HEADS, P_N_KEYS // n_i, n_i, n)
    return pl.pallas_call(
        functools.partial(_peer_dense_kernel, final=final),
        grid=(n // tn, n_e // eb),
        in_specs=[pl.BlockSpec((d, tn), lambda t, e: (0, t)),
                  pl.BlockSpec((eb, d), lambda t, e: (e, 0)),
                  pl.BlockSpec((d, eb), lambda t, e: (0, e)),
                  pl.BlockSpec((P_HEADS, P_N_KEYS // 2, tn), tok),
                  pl.BlockSpec((P_HEADS, 1, n_i, tn), rows_of),
                  pl.BlockSpec((P_HEADS, 1, n_i, tn), rows_of),
                  pl.BlockSpec((P_HEADS, P_N_KEYS // 2, tn), tok),
                  pl.BlockSpec((tn, d), lambda t, e: (t, 0)),
                  pl.BlockSpec((1, 1, d), lambda t, e: (t // per_b, 0, 0)),
                  pl.BlockSpec((1, d), lambda t, e: (0, 0))],
        out_specs=pl.BlockSpec((tn, d), lambda t, e: (t, 0)),
        out_shape=jax.ShapeDtypeStruct((n, d), F32),
        scratch_shapes=[pltpu.VMEM((d, tn), F32),
                        pltpu.VMEM((eb, tn), F32),
                        pltpu.VMEM((eb, tn), BF16),
                        pltpu.VMEM((eb // 2, tn), I32)],
        compiler_params=_params(("arbitrary", "arbitrary")),
        name="peer_dense",
    )(ht, u, vt, rank, by_rows(cnt), by_rows(a), b, x2, gate, fgain)


def _peer(x2, shift, scale, gate, w_q, sub_keys, u_tab, v_tab, fgain, seq, final):
    tn = min(512, seq)
    sub_keys = sub_keys.at[:, 1].set(sub_keys[:, 1][:, _PAIR_ORDER])
    ht, rank, cnt, a, b = _peer_route(x2, shift, scale, w_q.astype(BF16), sub_keys.astype(BF16), seq,
                                      min(256, seq))
    return _peer_dense(ht, u_tab.astype(BF16), v_tab.astype(BF16).T, rank, cnt, a, b, x2, gate, fgain,
                       seq, tn, P_EXPERT_BLOCK, final)


def kernel(x, c, a_w_in, a_kv_norm, a_w_uk, a_w_uv, a_w_out, b_w_in, b_conv, b_a_log, b_dt_bias,
           b_out_norm, b_w_out, p_w_q, p_sub_keys, p_u, p_v, ada_w, ada_b, final_norm):
    batch, seq, d = x.shape
    n = batch * seq
    depth = ada_w.shape[0]
    x2 = x.reshape(n, d)
    mod = _adaln(c, ada_w, ada_b).reshape(depth, batch, 6, 1, d)
    fgain = final_norm.reshape(1, d)
    tm = min(512, seq)
    for layer in range(depth):
        sh_m, sc_m, g_m, sh_f, sc_f, g_f = (mod[layer, :, k] for k in range(6))
        j = layer // 2
        if layer % 2 == 0:
            w_in = a_w_in[j]
            hr = A_HEADS * A_HEAD_DIM
            c0, c1, c2 = hr + A_KV_RANK, hr + A_KV_RANK + 512, hr + A_KV_RANK + 512 + A_IDX_DIM + A_IDX_HEADS
            w_a = jnp.concatenate([w_in[:, :hr], w_in[:, c0:c1], w_in[:, hr:c0], w_in[:, c1:c2],
                                   jnp.zeros((d, 1920 - c2), w_in.dtype)], axis=1).astype(BF16)
            proj = _norm_mod_matmul(x2, sh_m, sc_m, w_a, BF16, seq, tm, 1920)
            x2 = _dsa(proj, x2, g_m, a_kv_norm[j].reshape(1, -1),
                      a_w_uk[j].transpose(1, 2, 0).astype(BF16), a_w_uv[j].transpose(1, 0, 2).astype(BF16),
                      a_w_out[j].astype(BF16), batch, seq)
        else:
            w_in = b_w_in[j]
            nqkvz = 2 * B_K_HEADS * B_HEAD_DIM + 2 * B_V_HEADS * B_HEAD_DIM
            w_g = jnp.concatenate([w_in[:, nqkvz:], jnp.zeros((d, LANES - 2 * B_V_HEADS), w_in.dtype)], axis=1)
            qkvz = _norm_mod_matmul(x2, sh_m, sc_m, w_in[:, :nqkvz].astype(BF16), BF16, seq, tm, 2048)
            gates = _norm_mod_matmul(x2, sh_m, sc_m, w_g.astype(BF16), F32, seq, tm, LANES)
            gnat, gtr = _gdn_gates(gates, b_a_log[j], b_dt_bias[j], batch, seq)
            onorm = _gdn(qkvz, b_conv[j].reshape(B_CONV, -1), gnat, gtr, b_out_norm[j].reshape(1, -1), batch, seq)
            x2 = _matmul_residual(onorm, b_w_out[j].astype(BF16), x2, g_m, seq, tm)
        x2 = _peer(x2, sh_f, sc_f, g_f, p_w_q[layer], p_sub_keys[layer], p_u[layer], p_v[layer], fgain, seq,
                   final=(layer == depth - 1))
    return x2.reshape(batch, seq, d)
```
